```python
import jax, jax.numpy as jnp
from jax import lax
import numpy as np

D_MODEL = 1024
BATCH = 8
SEQ = 8192
DEPTH = 2

N_MIXERS = 2
POOL_WINDOWS = (2, 4, 8, 16)
N_POOL_GROUPS = len(POOL_WINDOWS)
POOL_GROUP_DIM = D_MODEL // N_POOL_GROUPS
CONV_WIDTH = 31
FFN_DIM = ((8 * D_MODEL // 3 + 255) // 256) * 256
FFN_CONV_WIDTH = 3
N_MOD = 6
EPS = 1e-6
N_POOL_LAYERS = (DEPTH + 1) // 2
N_CONV_LAYERS = DEPTH // 2

kernel_name = "hybrid_pool_conformer_convffn_trunk"


def rms_norm(x, g):
    xf = x.astype(jnp.float32)
    y = xf * lax.rsqrt(jnp.mean(xf * xf, axis=-1, keepdims=True) + EPS)
    return (y * g.astype(jnp.float32)).astype(x.dtype)


def layer_norm(x, g, b):
    xf = x.astype(jnp.float32)
    mu = jnp.mean(xf, axis=-1, keepdims=True)
    var = jnp.mean(jnp.square(xf - mu), axis=-1, keepdims=True)
    y = (xf - mu) * lax.rsqrt(var + EPS)
    return (y * g.astype(jnp.float32) + b.astype(jnp.float32)).astype(x.dtype)


def causal_depthwise_conv(x, w):
    k = w.shape[0]
    return lax.conv_general_dilated(
        x, w[:, None, :].astype(x.dtype), window_strides=(1,),
        padding=((k - 1, 0),), dimension_numbers=("NWC", "WIO", "NWC"),
        feature_group_count=x.shape[-1])


def causal_mean_pool(u, window):
    s = u.shape[1]
    cs = jnp.cumsum(u.astype(jnp.float32), axis=1)
    lag = jnp.pad(cs, ((0, 0), (window, 0), (0, 0)))[:, :s]
    cnt = jnp.minimum(jnp.arange(1, s + 1), window).astype(jnp.float32)
    return ((cs - lag) / cnt[None, :, None]).astype(u.dtype)


def pool_mixer(h, w_groups, scale):
    b, s, d = h.shape
    hg = h.reshape(b, s, N_POOL_GROUPS, POOL_GROUP_DIM)
    pooled = jnp.stack(
        [causal_mean_pool(hg[:, :, g], POOL_WINDOWS[g]) - hg[:, :, g]
         for g in range(N_POOL_GROUPS)], axis=2)
    y = jnp.einsum("bsgc,gcd->bsgd", pooled, w_groups).reshape(b, s, d)
    return y * scale


def conformer_conv_module(h, w_pw1, b_pw1, w_dw, b_dw, ln_g, ln_b, w_pw2, b_pw2):
    a = h @ w_pw1 + b_pw1
    val, gt = jnp.split(a, 2, axis=-1)
    u = val * jax.nn.sigmoid(gt)
    u = causal_depthwise_conv(u, w_dw) + b_dw
    u = jax.nn.silu(layer_norm(u, ln_g, ln_b))
    return u @ w_pw2 + b_pw2


def conv_ffn(h, w_up, w_dw, w_down):
    a = causal_depthwise_conv(h @ w_up, w_dw)
    g, v = jnp.split(a, 2, axis=-1)
    return (jax.nn.silu(g) * v) @ w_down


def setup_inputs(seed: int = 0) -> dict:
    key = jax.random.key(seed)
    ks = jax.random.split(key, 24)
    d, f = D_MODEL, FFN_DIM
    nrm = lambda k, shape, s: jax.random.normal(k, shape, jnp.float32) * s
    return {
        "x": nrm(ks[0], (BATCH, SEQ, d), 1.0),
        "c": nrm(ks[1], (BATCH, d), 1.0),
        "ada_w": nrm(ks[2], (DEPTH, d, N_MOD * d), 0.5 * d ** -0.5),
        "ada_b": nrm(ks[3], (DEPTH, N_MOD * d), 0.02),
        "pre_g": 1.0 + nrm(ks[4], (DEPTH, 2, d), 0.05),
        "post_g": 1.0 + nrm(ks[5], (DEPTH, 2, d), 0.05),
        "pool_w": nrm(ks[6], (N_POOL_LAYERS, N_POOL_GROUPS, POOL_GROUP_DIM, POOL_GROUP_DIM), POOL_GROUP_DIM ** -0.5),
        "pool_scale": 1.0 + nrm(ks[7], (N_POOL_LAYERS, d), 0.1),
        "cv_w_pw1": nrm(ks[8], (N_CONV_LAYERS, d, 2 * d), d ** -0.5),
        "cv_b_pw1": nrm(ks[9], (N_CONV_LAYERS, 2 * d), 0.02),
        "cv_w_dw": nrm(ks[10], (N_CONV_LAYERS, CONV_WIDTH, d), CONV_WIDTH ** -0.5),
        "cv_b_dw": nrm(ks[11], (N_CONV_LAYERS, d), 0.02),
        "cv_ln_g": 1.0 + nrm(ks[12], (N_CONV_LAYERS, d), 0.05),
        "cv_ln_b": nrm(ks[13], (N_CONV_LAYERS, d), 0.02),
        "cv_w_pw2": nrm(ks[14], (N_CONV_LAYERS, d, d), d ** -0.5),
        "cv_b_pw2": nrm(ks[15], (N_CONV_LAYERS, d), 0.02),
        "ffn_w_up": nrm(ks[16], (DEPTH, d, 2 * f), d ** -0.5),
        "ffn_w_dw": nrm(ks[17], (DEPTH, FFN_CONV_WIDTH, 2 * f), FFN_CONV_WIDTH ** -0.5),
        "ffn_w_down": nrm(ks[18], (DEPTH, f, d), f ** -0.5),
    }


def reference(x, c, ada_w, ada_b, pre_g, post_g, pool_w, pool_scale,
              cv_w_pw1, cv_b_pw1, cv_w_dw, cv_b_dw, cv_ln_g, cv_ln_b, cv_w_pw2, cv_b_pw2,
              ffn_w_up, ffn_w_dw, ffn_w_down):
    c_act = jax.nn.silu(c)
    for i in range(DEPTH):
        mod = c_act @ ada_w[i] + ada_b[i]
        sh_m, sc_m, gt_m, sh_f, sc_f, gt_f = [m[:, None, :] for m in jnp.split(mod, N_MOD, axis=-1)]

        h = rms_norm(x, pre_g[i, 0]) * (1.0 + sc_m) + sh_m
        j = i // N_MIXERS
        if i % N_MIXERS == 0:
            y = pool_mixer(h, pool_w[j], pool_scale[j])
        else:
            y = conformer_conv_module(h, cv_w_pw1[j], cv_b_pw1[j], cv_w_dw[j], cv_b_dw[j],
                                      cv_ln_g[j], cv_ln_b[j], cv_w_pw2[j], cv_b_pw2[j])
        x = x + gt_m * rms_norm(y, post_g[i, 0])

        h = rms_norm(x, pre_g[i, 1]) * (1.0 + sc_f) + sh_f
        y = conv_ffn(h, ffn_w_up[i], ffn_w_dw[i], ffn_w_down[i])
        x = x + gt_f * rms_norm(y, post_g[i, 1])
    return x
```

```python
import jax
import jax.numpy as jnp
from jax import lax
from jax.experimental import pallas as pl
from jax.experimental.pallas import tpu as pltpu

F32 = jnp.float32
BF16 = jnp.bfloat16

EPS = 1e-6
POOL_WINDOWS = (2, 4, 8, 16)
N_MOD = 6
SUBLANES = 8
LANES = 128
MXU_COLS = 256
VMEM_LIMIT_BYTES = 60 * 1024 * 1024

POOL_HALO = 16
CONV_HALO = 32
FFN_HALO = 8


def _rms_scale(x):
    return lax.rsqrt(jnp.mean(x * x, axis=-1, keepdims=True) + EPS)


def _rows(start, size):
    return pl.ds(start, size, stride=1)


def _lane_block(m, width=LANES):
    return slice(m * width, (m + 1) * width)


def _resident(shape):
    nd = len(shape)
    return pl.BlockSpec(shape, lambda b, s: (0,) * nd, pipeline_mode=pl.Buffered(1))


def _row_tile(ts, d):
    return pl.BlockSpec((1, ts, d), lambda b, s: (b, s, 0))


def _mod_spec(d):
    return pl.BlockSpec((1, N_MOD, d), lambda b, s: (b, 0, 0))


_SEQ_PARAMS = pltpu.CompilerParams(
    dimension_semantics=("arbitrary", "arbitrary"),
    vmem_limit_bytes=VMEM_LIMIT_BYTES)


def _mod_kernel(c_ref, w_ref, b_ref, o_ref):
    c = c_ref[...]
    ca = c * jax.nn.sigmoid(c)
    o_ref[0] = jnp.dot(ca, w_ref[0], preferred_element_type=F32,
                       precision=lax.Precision.HIGHEST) + b_ref[0]


def _modulation(c, ada_w, ada_b):
    depth, d, nd = ada_w.shape
    bsz = c.shape[0]
    out = pl.pallas_call(
        _mod_kernel,
        grid=(depth, nd // d),
        in_specs=[
            pl.BlockSpec((bsz, d), lambda i, j: (0, 0)),
            pl.BlockSpec((1, d, d), lambda i, j: (i, 0, j)),
            pl.BlockSpec((1, 1, d), lambda i, j: (i, 0, j)),
        ],
        out_specs=pl.BlockSpec((1, bsz, d), lambda i, j: (i, 0, j)),
        out_shape=jax.ShapeDtypeStruct((depth, bsz, nd), F32),
        name="adaln_mod",
    )(c, ada_w, ada_b.reshape(depth, 1, nd))
    return out.reshape(depth, bsz, N_MOD, d)


def _pool_kernel(x_ref, mod_ref, pre_g_ref, post_g_ref, w_ref, scale_ref,
                 o_ref, hbuf):
    s = pl.program_id(1)
    ts, d = x_ref.shape[1], x_ref.shape[2]
    gd = w_ref.shape[1]
    slabs_per_group = gd // LANES

    @pl.when(s == 0)
    def _():
        hbuf[:, 0:POOL_HALO, :] = jnp.zeros((d // LANES, POOL_HALO, LANES), F32)

    x = x_ref[0]
    m = mod_ref[0]
    sh, sc, gt = m[0:1], m[1:2], m[2:3]
    h = x * _rms_scale(x) * pre_g_ref[...] * (1.0 + sc) + sh
    for j in range(d // LANES):
        hbuf[j, POOL_HALO:POOL_HALO + ts, :] = h[:, _lane_block(j)]
    pos = lax.broadcasted_iota(jnp.int32, (ts, 1), 0) + (s * ts + 1)
    ys = []
    for g, w in enumerate(POOL_WINDOWS):
        inv_cnt = 1.0 / jnp.minimum(pos, w).astype(F32)
        pooled = []
        for j in range(g * slabs_per_group, (g + 1) * slabs_per_group):
            hj = hbuf[j, POOL_HALO:POOL_HALO + ts, :]
            acc = hj
            for k in range(1, w):
                acc = acc + hbuf[j, _rows(POOL_HALO - k, ts), :]
            pooled.append((acc * inv_cnt - hj).astype(BF16))
        ys.append(jnp.dot(jnp.concatenate(pooled, axis=-1), w_ref[g],
                          preferred_element_type=F32))
    y = jnp.concatenate(ys, axis=-1) * scale_ref[...]
    o_ref[0] = x_ref[0] + gt * (y * _rms_scale(y) * post_g_ref[...])
    for j in range(d // LANES):
        hbuf[j, 0:POOL_HALO, :] = hbuf[j, ts:ts + POOL_HALO, :]


def _pool_layer(x, mod, pre_g, post_g, w_groups, scale, *, ts):
    bsz, seq, d = x.shape
    g, gd, _ = w_groups.shape
    return pl.pallas_call(
        _pool_kernel,
        grid=(bsz, seq // ts),
        in_specs=[
            _row_tile(ts, d), _mod_spec(d), _resident((1, d)), _resident((1, d)),
            _resident((g, gd, gd)), _resident((1, d)),
        ],
        out_specs=_row_tile(ts, d),
        out_shape=jax.ShapeDtypeStruct(x.shape, x.dtype),
        scratch_shapes=[pltpu.VMEM((d // LANES, ts + POOL_HALO, LANES), F32)],
        compiler_params=_SEQ_PARAMS,
        name="pool_mixer",
    )(x, mod, pre_g.reshape(1, d), post_g.reshape(1, d),
      w_groups.astype(BF16), scale.reshape(1, d))


CONV_ROWS = 64


def _conf_kernel(x_ref, mod_ref, pre_g_ref, post_g_ref, w1_ref, b1_ref,
                 wdw_ref, bdw_ref, lng_ref, lnb_ref, w2_ref, b2_ref,
                 o_ref, ubuf, cbuf):
    s = pl.program_id(1)
    ts, d = x_ref.shape[1], x_ref.shape[2]
    kw = wdw_ref.shape[0]
    n_slabs = d // LANES

    @pl.when(s == 0)
    def _():
        ubuf[:, 0:CONV_HALO, :] = jnp.zeros((n_slabs, CONV_HALO, LANES), F32)

    x = x_ref[0]
    m = mod_ref[0]
    sh, sc, gt = m[0:1], m[1:2], m[2:3]
    h = (x * _rms_scale(x) * pre_g_ref[...] * (1.0 + sc) + sh).astype(BF16)
    for j in range(d // MXU_COLS):
        vcols = _lane_block(j, MXU_COLS)
        gcols = slice(d + j * MXU_COLS, d + (j + 1) * MXU_COLS)
        val = jnp.dot(h, w1_ref[:, vcols], preferred_element_type=F32) + b1_ref[:, vcols]
        gate = jnp.dot(h, w1_ref[:, gcols], preferred_element_type=F32) + b1_ref[:, gcols]
        u = val * jax.nn.sigmoid(gate)
        for i in range(MXU_COLS // LANES):
            ubuf[j * (MXU_COLS // LANES) + i, CONV_HALO:CONV_HALO + ts, :] = u[:, _lane_block(i)]

    base = CONV_HALO - (kw - 1)
    for j in range(n_slabs):
        cols = _lane_block(j)
        taps = [jnp.broadcast_to(wdw_ref[k:k + 1, cols], (SUBLANES, LANES))
                for k in range(kw)]
        bias = jnp.broadcast_to(bdw_ref[:, cols], (SUBLANES, LANES))

        def body(i, carry, j=j, cols=cols, taps=taps, bias=bias):
            r0 = pl.multiple_of(i * CONV_ROWS, CONV_ROWS)
            for r in range(0, CONV_ROWS, SUBLANES):
                acc = bias
                for k in range(kw):
                    acc = acc + taps[k] * ubuf[j, _rows(r0 + (base + r + k), SUBLANES), :]
                cbuf[pl.ds(r0 + r, SUBLANES), cols] = acc
            return carry

        lax.fori_loop(0, ts // CONV_ROWS, body, 0)

    v = cbuf[...]
    mu = jnp.mean(v, axis=-1, keepdims=True)
    vc = v - mu
    var = jnp.mean(vc * vc, axis=-1, keepdims=True)
    ln = vc * lax.rsqrt(var + EPS) * lng_ref[...] + lnb_ref[...]
    act = (ln * jax.nn.sigmoid(ln)).astype(BF16)
    y = jnp.dot(act, w2_ref[...], preferred_element_type=F32) + b2_ref[...]
    o_ref[0] = x_ref[0] + gt * (y * _rms_scale(y) * post_g_ref[...])
    for j in range(n_slabs):
        ubuf[j, 0:CONV_HALO, :] = ubuf[j, ts:ts + CONV_HALO, :]


def _conf_layer(x, mod, pre_g, post_g, w1, b1, wdw, bdw, lng, lnb, w2, b2, *, ts):
    bsz, seq, d = x.shape
    kw = wdw.shape[0]
    return pl.pallas_call(
        _conf_kernel,
        grid=(bsz, seq // ts),
        in_specs=[
            _row_tile(ts, d), _mod_spec(d), _resident((1, d)), _resident((1, d)),
            _resident((d, 2 * d)), _resident((1, 2 * d)),
            _resident((kw, d)), _resident((1, d)), _resident((1, d)),
            _resident((1, d)), _resident((d, d)), _resident((1, d)),
        ],
        out_specs=_row_tile(ts, d),
        out_shape=jax.ShapeDtypeStruct(x.shape, x.dtype),
        scratch_shapes=[pltpu.VMEM((d // LANES, ts + CONV_HALO, LANES), F32),
                        pltpu.VMEM((ts, d), F32)],
        compiler_params=_SEQ_PARAMS,
        name="conformer_conv",
    )(x, mod, pre_g.reshape(1, d), post_g.reshape(1, d),
      w1.astype(BF16), b1.reshape(1, 2 * d), wdw, bdw.reshape(1, d),
      lng.reshape(1, d), lnb.reshape(1, d), w2.astype(BF16), b2.reshape(1, d))


FFN_CHUNK = 256


def _ffn_kernel(x_ref, mod_ref, pre_g_ref, post_g_ref, wup_ref, wdw_ref,
                wdn_ref, o_ref, ubuf):
    s = pl.program_id(1)
    ts = x_ref.shape[1]
    f = wdn_ref.shape[0]
    kw = wdw_ref.shape[0]
    fc = FFN_CHUNK
    slabs = fc // LANES

    @pl.when(s == 0)
    def _():
        ubuf[:, 0:FFN_HALO, :] = jnp.zeros((ubuf.shape[0], FFN_HALO, LANES), F32)

    x = x_ref[0]
    m = mod_ref[0]
    sh, sc, gt = m[3:4], m[4:5], m[5:6]
    h = (x * _rms_scale(x) * pre_g_ref[...] * (1.0 + sc) + sh).astype(BF16)

    def conv_cols(off):
        u = jnp.dot(h, wup_ref[:, off:off + fc], preferred_element_type=F32)
        outs = []
        for i in range(slabs):
            slab = off // LANES + i
            cols = slice(off + i * LANES, off + (i + 1) * LANES)
            ui = u[:, _lane_block(i)]
            ubuf[slab, FFN_HALO:FFN_HALO + ts, :] = ui
            out = wdw_ref[kw - 1:kw, cols] * ui
            for k in range(kw - 1):
                lo = FFN_HALO - (kw - 1) + k
                out = out + wdw_ref[k:k + 1, cols] * ubuf[slab, _rows(lo, ts), :]
            ubuf[slab, 0:FFN_HALO, :] = ubuf[slab, ts:ts + FFN_HALO, :]
            outs.append(out)
        return jnp.concatenate(outs, axis=-1)

    y = None
    for j in range(f // fc):
        gate = conv_cols(j * fc)
        val = conv_cols(f + j * fc)
        act = (gate * jax.nn.sigmoid(gate) * val).astype(BF16)
        part = jnp.dot(act, wdn_ref[j * fc:(j + 1) * fc, :],
                       preferred_element_type=F32)
        y = part if y is None else y + part
    o_ref[0] = x_ref[0] + gt * (y * _rms_scale(y) * post_g_ref[...])


def _ffn_layer(x, mod, pre_g, post_g, w_up, w_dw, w_down, *, ts):
    bsz, seq, d = x.shape
    f = w_down.shape[0]
    kw = w_dw.shape[0]
    return pl.pallas_call(
        _ffn_kernel,
        grid=(bsz, seq // ts),
        in_specs=[
            _row_tile(ts, d), _mod_spec(d), _resident((1, d)), _resident((1, d)),
            _resident((d, 2 * f)), _resident((kw, 2 * f)), _resident((f, d)),
        ],
        out_specs=_row_tile(ts, d),
        out_shape=jax.ShapeDtypeStruct(x.shape, x.dtype),
        scratch_shapes=[pltpu.VMEM((2 * f // LANES, ts + FFN_HALO, LANES), F32)],
        compiler_params=_SEQ_PARAMS,
        name="conv_ffn",
    )(x, mod, pre_g.reshape(1, d), post_g.reshape(1, d),
      w_up.astype(BF16), w_dw, w_down.astype(BF16))


def kernel(x, c, ada_w, ada_b, pre_g, post_g, pool_w, pool_scale, cv_w_pw1, cv_b_pw1, cv_w_dw, cv_b_dw, cv_ln_g, cv_ln_b, cv_w_pw2, cv_b_pw2, ffn_w_up, ffn_w_dw, ffn_w_down):
    depth = ada_w.shape[0]
    mod = _modulation(c, ada_w, ada_b)
    for i in range(depth):
        j = i // 2
        if i % 2 == 0:
            x = _pool_layer(x, mod[i], pre_g[i, 0], post_g[i, 0],
                            pool_w[j], pool_scale[j], ts=256)
        else:
            x = _conf_layer(x, mod[i], pre_g[i, 0], post_g[i, 0],
                            cv_w_pw1[j], cv_b_pw1[j], cv_w_dw[j], cv_b_dw[j],
                            cv_ln_g[j], cv_ln_b[j], cv_w_pw2[j], cv_b_pw2[j],
                            ts=256)
        x = _ffn_layer(x, mod[i], pre_g[i, 1], post_g[i, 1],
                       ffn_w_up[i], ffn_w_dw[i], ffn_w_down[i], ts=512)
    return x
```

```python
import jax
import jax.numpy as jnp
from jax import lax
from jax.experimental import pallas as pl
from jax.experimental.pallas import tpu as pltpu

F32 = jnp.float32
BF16 = jnp.bfloat16

EPS = 1e-6
POOL_WINDOWS = (2, 4, 8, 16)
N_MOD = 6
SUBLANES = 8
LANES = 128
MXU_COLS = 256
VMEM_LIMIT_BYTES = 60 * 1024 * 1024

POOL_HALO = 32
CONV_HALO = 32
FFN_HALO = 8


def _rms_scale(x):
    return lax.rsqrt(jnp.mean(x * x, axis=-1, keepdims=True) + EPS)


def _rows(start, size):
    return pl.ds(start, size, stride=1)


def _lane_block(m, width=LANES):
    return slice(m * width, (m + 1) * width)


def _resident(shape):
    nd = len(shape)
    return pl.BlockSpec(shape, lambda b, s: (0,) * nd, pipeline_mode=pl.Buffered(1))


def _row_tile(ts, d):
    return pl.BlockSpec((1, ts, d), lambda b, s: (b, s, 0))


def _mod_spec(d):
    return pl.BlockSpec((1, N_MOD, d), lambda b, s: (b, 0, 0))


_SEQ_PARAMS = pltpu.CompilerParams(
    dimension_semantics=("arbitrary", "arbitrary"),
    vmem_limit_bytes=VMEM_LIMIT_BYTES)


def _mod_kernel(c_ref, w_ref, b_ref, o_ref):
    c = c_ref[...]
    ca = c * jax.nn.sigmoid(c)
    o_ref[0] = jnp.dot(ca, w_ref[0], preferred_element_type=F32,
                       precision=lax.Precision.HIGHEST) + b_ref[0]


def _modulation(c, ada_w, ada_b):
    depth, d, nd = ada_w.shape
    bsz = c.shape[0]
    out = pl.pallas_call(
        _mod_kernel,
        grid=(depth, nd // d),
        in_specs=[
            pl.BlockSpec((bsz, d), lambda i, j: (0, 0)),
            pl.BlockSpec((1, d, d), lambda i, j: (i, 0, j)),
            pl.BlockSpec((1, 1, d), lambda i, j: (i, 0, j)),
        ],
        out_specs=pl.BlockSpec((1, bsz, d), lambda i, j: (i, 0, j)),
        out_shape=jax.ShapeDtypeStruct((depth, bsz, nd), F32),
        name="adaln_mod",
    )(c, ada_w, ada_b.reshape(depth, 1, nd))
    return out.reshape(depth, bsz, N_MOD, d)


def _pool_kernel(x_ref, mod_ref, pre_g_ref, post_g_ref, w_ref, scale_ref,
                 o_ref, hbuf, wbuf):
    s = pl.program_id(1)
    ts, d = x_ref.shape[1], x_ref.shape[2]
    gd = w_ref.shape[1]
    slabs_per_group = gd // LANES
    ext = POOL_HALO + ts

    @pl.when(s == 0)
    def _():
        hbuf[:, 0:POOL_HALO, :] = jnp.zeros((d // LANES, POOL_HALO, LANES), F32)

    x = x_ref[0]
    m = mod_ref[0]
    sh, sc, gt = m[0:1], m[1:2], m[2:3]
    h = x * _rms_scale(x) * (pre_g_ref[...] * (1.0 + sc)) + sh
    for j in range(d // LANES):
        hbuf[j, POOL_HALO:ext, :] = h[:, _lane_block(j)]
    pos = lax.broadcasted_iota(jnp.int32, (ts, LANES), 0) + (s * ts + 1)
    ys = []
    for g, w in enumerate(POOL_WINDOWS):
        inv_cnt = 1.0 / jnp.minimum(pos, w).astype(F32)
        pooled = []
        for j in range(g * slabs_per_group, (g + 1) * slabs_per_group):
            src, shift, lo = hbuf, 1, SUBLANES
            while 2 * shift < w:
                wbuf[j, lo:ext, :] = (src[j, lo:ext, :]
                                      + src[j, _rows(lo - shift, ext - lo), :])
                src, shift, lo = wbuf, 2 * shift, lo + SUBLANES
            acc = (src[j, POOL_HALO:ext, :]
                   + src[j, _rows(POOL_HALO - shift, ts), :])
            pooled.append((acc * inv_cnt - hbuf[j, POOL_HALO:ext, :]).astype(BF16))
        ys.append(jnp.dot(jnp.concatenate(pooled, axis=-1), w_ref[g],
                          preferred_element_type=F32))
    y = jnp.concatenate(ys, axis=-1) * scale_ref[...]
    o_ref[0] = x_ref[0] + y * _rms_scale(y) * (gt * post_g_ref[...])
    for j in range(d // LANES):
        hbuf[j, 0:POOL_HALO, :] = hbuf[j, ts:ext, :]


def _pool_layer(x, mod, pre_g, post_g, w_groups, scale, *, ts):
    bsz, seq, d = x.shape
    g, gd, _ = w_groups.shape
    return pl.pallas_call(
        _pool_kernel,
        grid=(bsz, seq // ts),
        in_specs=[
            _row_tile(ts, d), _mod_spec(d), _resident((1, d)), _resident((1, d)),
            _resident((g, gd, gd)), _resident((1, d)),
        ],
        out_specs=_row_tile(ts, d),
        out_shape=jax.ShapeDtypeStruct(x.shape, x.dtype),
        scratch_shapes=[pltpu.VMEM((d // LANES, ts + POOL_HALO, LANES), F32)] * 2,
        compiler_params=_SEQ_PARAMS,
        name="pool_mixer",
    )(x, mod, pre_g.reshape(1, d), post_g.reshape(1, d),
      w_groups.astype(BF16), scale.reshape(1, d))


CONV_GROUPS = 4
PACK = 2
GROUP_ROWS = PACK * SUBLANES
CONV_TAIL = SUBLANES


def _conf_kernel(x_ref, mod_ref, pre_g_ref, post_g_ref, w1_ref, b1_ref,
                 taps_ref, bdw_ref, lng_ref, lnb_ref, w2_ref, b2_ref,
                 o_ref, ubuf, even, odd, cbuf):
    s = pl.program_id(1)
    ts, d = x_ref.shape[1], x_ref.shape[2]
    kw = taps_ref.shape[1] // GROUP_ROWS
    n_slabs = d // LANES
    ext = CONV_HALO + ts
    block_rows = CONV_GROUPS * GROUP_ROWS

    @pl.when(s == 0)
    def _():
        ubuf[:, 0:CONV_HALO, :] = jnp.zeros((n_slabs, CONV_HALO, LANES), F32)
        ubuf[:, ext:ext + CONV_TAIL, :] = jnp.zeros((n_slabs, CONV_TAIL, LANES), F32)

    x = x_ref[0]
    m = mod_ref[0]
    sh, sc, gt = m[0:1], m[1:2], m[2:3]
    h = (x * _rms_scale(x) * (pre_g_ref[...] * (1.0 + sc)) + sh).astype(BF16)
    for j in range(d // MXU_COLS):
        vcols = _lane_block(j, MXU_COLS)
        gcols = slice(d + j * MXU_COLS, d + (j + 1) * MXU_COLS)
        val = jnp.dot(h, w1_ref[:, vcols], preferred_element_type=F32) + b1_ref[:, vcols]
        gate = jnp.dot(h, w1_ref[:, gcols], preferred_element_type=F32) + b1_ref[:, gcols]
        u = val * jax.nn.sigmoid(gate)
        for i in range(MXU_COLS // LANES):
            ubuf[j * (MXU_COLS // LANES) + i, CONV_HALO:ext, :] = u[:, _lane_block(i)]

    base = CONV_HALO - (kw - 1)
    for j in range(n_slabs):
        cols = _lane_block(j)
        even[j] = pltpu.bitcast(ubuf[j, 0:ext, :].astype(BF16), jnp.uint32)
        odd[j] = pltpu.bitcast(ubuf[j, _rows(1, ext), :].astype(BF16), jnp.uint32)
        bias = jnp.broadcast_to(bdw_ref[:, cols], (GROUP_ROWS, LANES))

        def body(i, carry, j=j, cols=cols, bias=bias):
            r0 = pl.multiple_of(i * block_rows, block_rows)
            w0 = pl.multiple_of(i * (block_rows // PACK), block_rows // PACK)
            accs = [None] * CONV_GROUPS
            windows = {}

            def window(first):
                if first not in windows:
                    src = even if first % PACK == 0 else odd
                    win = src[j, _rows(w0 + first // PACK, SUBLANES), :]
                    windows[first] = pltpu.bitcast(win, BF16)
                return windows[first]

            for k in range(kw):
                tap = taps_ref[j, k * GROUP_ROWS:(k + 1) * GROUP_ROWS, :]
                for g in range(CONV_GROUPS):
                    win = window(base + g * GROUP_ROWS + k)
                    prod = win.astype(F32) * tap.astype(F32)
                    accs[g] = prod if accs[g] is None else accs[g] + prod
            for g in range(CONV_GROUPS):
                cbuf[pl.ds(r0 + g * GROUP_ROWS, GROUP_ROWS), cols] = accs[g] + bias
            return carry

        lax.fori_loop(0, ts // block_rows, body, 0)

    v = cbuf[...]
    mu = jnp.mean(v, axis=-1, keepdims=True)
    vc = v - mu
    var = jnp.mean(vc * vc, axis=-1, keepdims=True)
    ln = vc * lax.rsqrt(var + EPS) * lng_ref[...] + lnb_ref[...]
    act = (ln * jax.nn.sigmoid(ln)).astype(BF16)
    y = jnp.dot(act, w2_ref[...], preferred_element_type=F32) + b2_ref[...]
    o_ref[0] = x_ref[0] + y * _rms_scale(y) * (gt * post_g_ref[...])
    for j in range(n_slabs):
        ubuf[j, 0:CONV_HALO, :] = ubuf[j, ts:ext, :]


def _conf_layer(x, mod, pre_g, post_g, w1, b1, wdw, bdw, lng, lnb, w2, b2, *, ts):
    bsz, seq, d = x.shape
    kw = wdw.shape[0]
    n_slabs = d // LANES
    ext = CONV_HALO + ts
    taps = wdw.astype(BF16).reshape(kw, n_slabs, 1, LANES).transpose(1, 0, 2, 3)
    taps = jnp.broadcast_to(taps, (n_slabs, kw, GROUP_ROWS, LANES))
    taps = taps.reshape(n_slabs, kw * GROUP_ROWS, LANES)
    return pl.pallas_call(
        _conf_kernel,
        grid=(bsz, seq // ts),
        in_specs=[
            _row_tile(ts, d), _mod_spec(d), _resident((1, d)), _resident((1, d)),
            _resident((d, 2 * d)), _resident((1, 2 * d)),
            _resident(taps.shape), _resident((1, d)), _resident((1, d)),
            _resident((1, d)), _resident((d, d)), _resident((1, d)),
        ],
        out_specs=_row_tile(ts, d),
        out_shape=jax.ShapeDtypeStruct(x.shape, x.dtype),
        scratch_shapes=[pltpu.VMEM((n_slabs, ext + CONV_TAIL, LANES), F32),
                        pltpu.VMEM((n_slabs, ext // PACK, LANES), jnp.uint32),
                        pltpu.VMEM((n_slabs, ext // PACK, LANES), jnp.uint32),
                        pltpu.VMEM((ts, d), F32)],
        compiler_params=_SEQ_PARAMS,
        name="conformer_conv",
    )(x, mod, pre_g.reshape(1, d), post_g.reshape(1, d),
      w1.astype(BF16), b1.reshape(1, 2 * d), taps, bdw.reshape(1, d),
      lng.reshape(1, d), lnb.reshape(1, d), w2.astype(BF16), b2.reshape(1, d))


FFN_CHUNK = 256
FFN_ROW_SPLITS = 1
FFN_TAIL_CHUNKS = 3


def _ffn_kernel(x_ref, mod_ref, pre_g_ref, post_g_ref, wup_ref, wdw_ref,
                wdn_ref, o_ref, ubuf, abuf):
    s = pl.program_id(1)
    ts = x_ref.shape[1]
    f = wdn_ref.shape[0]
    kw = wdw_ref.shape[0]
    fc = FFN_CHUNK
    slabs = fc // LANES
    n_chunks = f // fc

    @pl.when(s == 0)
    def _():
        ubuf[:, 0:FFN_HALO, :] = jnp.zeros((ubuf.shape[0], FFN_HALO, LANES), F32)

    m = mod_ref[0]
    sh, sc, gt = m[3:4], m[4:5], m[5:6]
    in_gain = pre_g_ref[...] * (1.0 + sc)
    out_gain = gt * post_g_ref[...]
    rs = ts // FFN_ROW_SPLITS

    for r0 in range(0, ts, rs):
        last = r0 + rs == ts
        x = x_ref[0, r0:r0 + rs, :]
        h = (x * _rms_scale(x) * in_gain + sh).astype(BF16)

        def up(j):
            for off in (j * fc, f + j * fc):
                u = jnp.dot(h, wup_ref[:, off:off + fc], preferred_element_type=F32)
                for i in range(slabs):
                    ubuf[off // LANES + i, FFN_HALO + r0:FFN_HALO + r0 + rs, :] = (
                        u[:, _lane_block(i)])

        def conv(slab):
            cols = _lane_block(slab)
            out = None
            for k in range(kw):
                lo = FFN_HALO - (kw - 1) + k + r0
                term = wdw_ref[k:k + 1, cols] * ubuf[slab, _rows(lo, rs), :]
                out = term if out is None else out + term
            if last:
                ubuf[slab, 0:FFN_HALO, :] = ubuf[slab, ts:ts + FFN_HALO, :]
            return out

        def act(j):
            for i in range(slabs):
                gate = conv(j * slabs + i)
                val = conv((f + j * fc) // LANES + i)
                cols = _lane_block(j * slabs + i)
                abuf[r0:r0 + rs, cols] = (gate * jax.nn.sigmoid(gate) * val).astype(BF16)

        up(0)
        for j in range(n_chunks):
            if j + 1 < n_chunks:
                up(j + 1)
            act(j)
        split = (n_chunks - FFN_TAIL_CHUNKS) * fc
        y = jnp.dot(abuf[r0:r0 + rs, :split], wdn_ref[:split, :],
                    preferred_element_type=F32)
        y = y + jnp.dot(abuf[r0:r0 + rs, split:], wdn_ref[split:, :],
                        preferred_element_type=F32)
        o_ref[0, r0:r0 + rs, :] = (
            x_ref[0, r0:r0 + rs, :] + y * _rms_scale(y) * out_gain)


def _ffn_layer(x, mod, pre_g, post_g, w_up, w_dw, w_down, *, ts):
    bsz, seq, d = x.shape
    f = w_down.shape[0]
    kw = w_dw.shape[0]
    return pl.pallas_call(
        _ffn_kernel,
        grid=(bsz, seq // ts),
        in_specs=[
            _row_tile(ts, d), _mod_spec(d), _resident((1, d)), _resident((1, d)),
            _resident((d, 2 * f)), _resident((kw, 2 * f)), _resident((f, d)),
        ],
        out_specs=_row_tile(ts, d),
        out_shape=jax.ShapeDtypeStruct(x.shape, x.dtype),
        scratch_shapes=[pltpu.VMEM((2 * f // LANES, ts + FFN_HALO, LANES), F32),
                        pltpu.VMEM((ts, f), BF16)],
        compiler_params=_SEQ_PARAMS,
        name="conv_ffn",
    )(x, mod, pre_g.reshape(1, d), post_g.reshape(1, d),
      w_up.astype(BF16), w_dw, w_down.astype(BF16))


def kernel(x, c, ada_w, ada_b, pre_g, post_g, pool_w, pool_scale, cv_w_pw1, cv_b_pw1, cv_w_dw, cv_b_dw, cv_ln_g, cv_ln_b, cv_w_pw2, cv_b_pw2, ffn_w_up, ffn_w_dw, ffn_w_down):
    depth = ada_w.shape[0]
    mod = _modulation(c, ada_w, ada_b)
    for i in range(depth):
        j = i // 2
        if i % 2 == 0:
            x = _pool_layer(x, mod[i], pre_g[i, 0], post_g[i, 0],
                            pool_w[j], pool_scale[j], ts=512)
        else:
            x = _conf_layer(x, mod[i], pre_g[i, 0], post_g[i, 0],
                            cv_w_pw1[j], cv_b_pw1[j], cv_w_dw[j], cv_b_dw[j],
                            cv_ln_g[j], cv_ln_b[j], cv_w_pw2[j], cv_b_pw2[j],
                            ts=512)
        x = _ffn_layer(x, mod[i], pre_g[i, 1], post_g[i, 1],
                       ffn_w_up[i], ffn_w_dw[i], ffn_w_down[i], ts=512)
    return x
```

```python
import jax
import jax.numpy as jnp
from jax import lax
from jax.experimental import pallas as pl
from jax.experimental.pallas import tpu as pltpu

F32 = jnp.float32
BF16 = jnp.bfloat16

EPS = 1e-6
POOL_WINDOWS = (2, 4, 8, 16)
N_MOD = 6
SUBLANES = 8
LANES = 128
MXU_COLS = 256
VMEM_LIMIT_BYTES = 60 * 1024 * 1024

POOL_HALO = 32
CONV_HALO = 32
FFN_HALO = 8


def _rms_scale(x):
    return lax.rsqrt(jnp.mean(x * x, axis=-1, keepdims=True) + EPS)


def _rows(start, size):
    return pl.ds(start, size, stride=1)


def _lane_block(m, width=LANES):
    return slice(m * width, (m + 1) * width)


def _resident(shape):
    nd = len(shape)
    return pl.BlockSpec(shape, lambda b, s: (0,) * nd, pipeline_mode=pl.Buffered(1))


def _row_tile(ts, d):
    return pl.BlockSpec((1, ts, d), lambda b, s: (b, s, 0))


def _mod_spec(d):
    return pl.BlockSpec((1, N_MOD, d), lambda b, s: (b, 0, 0))


_SEQ_PARAMS = pltpu.CompilerParams(
    dimension_semantics=("arbitrary", "arbitrary"),
    vmem_limit_bytes=VMEM_LIMIT_BYTES)


def _mod_kernel(c_ref, w_ref, b_ref, o_ref):
    c = c_ref[...]
    ca = c * jax.nn.sigmoid(c)
    o_ref[0] = jnp.dot(ca, w_ref[0], preferred_element_type=F32,
                       precision=lax.Precision.HIGHEST) + b_ref[0]


def _modulation(c, ada_w, ada_b):
    depth, d, nd = ada_w.shape
    bsz = c.shape[0]
    out = pl.pallas_call(
        _mod_kernel,
        grid=(depth, nd // d),
        in_specs=[
            pl.BlockSpec((bsz, d), lambda i, j: (0, 0)),
            pl.BlockSpec((1, d, d), lambda i, j: (i, 0, j)),
            pl.BlockSpec((1, 1, d), lambda i, j: (i, 0, j)),
        ],
        out_specs=pl.BlockSpec((1, bsz, d), lambda i, j: (i, 0, j)),
        out_shape=jax.ShapeDtypeStruct((depth, bsz, nd), F32),
        name="adaln_mod",
    )(c, ada_w, ada_b.reshape(depth, 1, nd))
    return out.reshape(depth, bsz, N_MOD, d)


def _pool_kernel(x_ref, mod_ref, pre_g_ref, post_g_ref, w_ref, scale_ref,
                 o_ref, hbuf, wbuf):
    s = pl.program_id(1)
    ts, d = x_ref.shape[1], x_ref.shape[2]
    gd = w_ref.shape[1]
    slabs_per_group = gd // LANES
    ext = POOL_HALO + ts

    @pl.when(s == 0)
    def _():
        hbuf[:, 0:POOL_HALO, :] = jnp.zeros((d // LANES, POOL_HALO, LANES), F32)

    x = x_ref[0]
    m = mod_ref[0]
    sh, sc, gt = m[0:1], m[1:2], m[2:3]
    h = x * _rms_scale(x) * (pre_g_ref[...] * (1.0 + sc)) + sh
    for j in range(d // LANES):
        hbuf[j, POOL_HALO:ext, :] = h[:, _lane_block(j)]
    pos = lax.broadcasted_iota(jnp.int32, (ts, LANES), 0) + (s * ts + 1)
    ys = []
    for g, w in enumerate(POOL_WINDOWS):
        inv_cnt = 1.0 / jnp.minimum(pos, w).astype(F32)
        pooled = []
        for j in range(g * slabs_per_group, (g + 1) * slabs_per_group):
            src, shift, lo = hbuf, 1, SUBLANES
            while 2 * shift < w:
                wbuf[j, lo:ext, :] = (src[j, lo:ext, :]
                                      + src[j, _rows(lo - shift, ext - lo), :])
                src, shift, lo = wbuf, 2 * shift, lo + SUBLANES
            acc = (src[j, POOL_HALO:ext, :]
                   + src[j, _rows(POOL_HALO - shift, ts), :])
            pooled.append((acc * inv_cnt - hbuf[j, POOL_HALO:ext, :]).astype(BF16))
        ys.append(jnp.dot(jnp.concatenate(pooled, axis=-1), w_ref[g],
                          preferred_element_type=F32))
    y = jnp.concatenate(ys, axis=-1) * scale_ref[...]
    o_ref[0] = x_ref[0] + y * _rms_scale(y) * (gt * post_g_ref[...])
    for j in range(d // LANES):
        hbuf[j, 0:POOL_HALO, :] = hbuf[j, ts:ext, :]


def _pool_layer(x, mod, pre_g, post_g, w_groups, scale, *, ts):
    bsz, seq, d = x.shape
    g, gd, _ = w_groups.shape
    return pl.pallas_call(
        _pool_kernel,
        grid=(bsz, seq // ts),
        in_specs=[
            _row_tile(ts, d), _mod_spec(d), _resident((1, d)), _resident((1, d)),
            _resident((g, gd, gd)), _resident((1, d)),
        ],
        out_specs=_row_tile(ts, d),
        out_shape=jax.ShapeDtypeStruct(x.shape, x.dtype),
        scratch_shapes=[pltpu.VMEM((d // LANES, ts + POOL_HALO, LANES), F32)] * 2,
        compiler_params=_SEQ_PARAMS,
        name="pool_mixer",
    )(x, mod, pre_g.reshape(1, d), post_g.reshape(1, d),
      w_groups.astype(BF16), scale.reshape(1, d))


CONV_GROUPS = 4
PACK = 2
GROUP_ROWS = PACK * SUBLANES
CONV_TAIL = SUBLANES


def _conf_kernel(x_ref, mod_ref, pre_g_ref, post_g_ref, w1_ref, b1_ref,
                 taps_ref, bdw_ref, lng_ref, lnb_ref, w2_ref, b2_ref,
                 o_ref, ubuf, even, odd, cbuf):
    s = pl.program_id(1)
    ts, d = x_ref.shape[1], x_ref.shape[2]
    kw = taps_ref.shape[1] // GROUP_ROWS
    n_slabs = d // LANES
    ext = CONV_HALO + ts
    block_rows = CONV_GROUPS * GROUP_ROWS

    @pl.when(s == 0)
    def _():
        ubuf[:, 0:CONV_HALO, :] = jnp.zeros((n_slabs, CONV_HALO, LANES), F32)
        ubuf[:, ext:ext + CONV_TAIL, :] = jnp.zeros((n_slabs, CONV_TAIL, LANES), F32)

    x = x_ref[0]
    m = mod_ref[0]
    sh, sc, gt = m[0:1], m[1:2], m[2:3]
    h = (x * _rms_scale(x) * (pre_g_ref[...] * (1.0 + sc)) + sh).astype(BF16)
    for j in range(d // MXU_COLS):
        vcols = _lane_block(j, MXU_COLS)
        gcols = slice(d + j * MXU_COLS, d + (j + 1) * MXU_COLS)
        val = jnp.dot(h, w1_ref[:, vcols], preferred_element_type=F32) + b1_ref[:, vcols]
        gate = jnp.dot(h, w1_ref[:, gcols], preferred_element_type=F32) + b1_ref[:, gcols]
        u = val * jax.nn.sigmoid(gate)
        for i in range(MXU_COLS // LANES):
            ubuf[j * (MXU_COLS // LANES) + i, CONV_HALO:ext, :] = u[:, _lane_block(i)]

    base = CONV_HALO - (kw - 1)
    for j in range(n_slabs):
        cols = _lane_block(j)
        even[j] = pltpu.bitcast(ubuf[j, 0:ext, :].astype(BF16), jnp.uint32)
        odd[j] = pltpu.bitcast(ubuf[j, _rows(1, ext), :].astype(BF16), jnp.uint32)
        bias = jnp.broadcast_to(bdw_ref[:, cols], (GROUP_ROWS, LANES))

        def body(i, carry, j=j, cols=cols, bias=bias):
            r0 = pl.multiple_of(i * block_rows, block_rows)
            w0 = pl.multiple_of(i * (block_rows // PACK), block_rows // PACK)
            accs = [None] * CONV_GROUPS
            windows = {}

            def window(first):
                if first not in windows:
                    src = even if first % PACK == 0 else odd
                    win = src[j, _rows(w0 + first // PACK, SUBLANES), :]
                    windows[first] = pltpu.bitcast(win, BF16)
                return windows[first]

            for k in range(kw):
                tap = taps_ref[j, k * GROUP_ROWS:(k + 1) * GROUP_ROWS, :]
                for g in range(CONV_GROUPS):
                    win = window(base + g * GROUP_ROWS + k)
                    prod = win.astype(F32) * tap.astype(F32)
                    accs[g] = prod if accs[g] is None else accs[g] + prod
            for g in range(CONV_GROUPS):
                cbuf[pl.ds(r0 + g * GROUP_ROWS, GROUP_ROWS), cols] = accs[g] + bias
            return carry

        lax.fori_loop(0, ts // block_rows, body, 0)

    v = cbuf[...]
    mu = jnp.mean(v, axis=-1, keepdims=True)
    vc = v - mu
    var = jnp.mean(vc * vc, axis=-1, keepdims=True)
    ln = vc * lax.rsqrt(var + EPS) * lng_ref[...] + lnb_ref[...]
    act = (ln * jax.nn.sigmoid(ln)).astype(BF16)
    y = jnp.dot(act, w2_ref[...], preferred_element_type=F32) + b2_ref[...]
    o_ref[0] = x_ref[0] + y * _rms_scale(y) * (gt * post_g_ref[...])
    for j in range(n_slabs):
        ubuf[j, 0:CONV_HALO, :] = ubuf[j, ts:ext, :]


def _conf_layer(x, mod, pre_g, post_g, w1, b1, wdw, bdw, lng, lnb, w2, b2, *, ts):
    bsz, seq, d = x.shape
    kw = wdw.shape[0]
    n_slabs = d // LANES
    ext = CONV_HALO + ts
    taps = wdw.astype(BF16).reshape(kw, n_slabs, 1, LANES).transpose(1, 0, 2, 3)
    taps = jnp.broadcast_to(taps, (n_slabs, kw, GROUP_ROWS, LANES))
    taps = taps.reshape(n_slabs, kw * GROUP_ROWS, LANES)
    return pl.pallas_call(
        _conf_kernel,
        grid=(bsz, seq // ts),
        in_specs=[
            _row_tile(ts, d), _mod_spec(d), _resident((1, d)), _resident((1, d)),
            _resident((d, 2 * d)), _resident((1, 2 * d)),
            _resident(taps.shape), _resident((1, d)), _resident((1, d)),
            _resident((1, d)), _resident((d, d)), _resident((1, d)),
        ],
        out_specs=_row_tile(ts, d),
        out_shape=jax.ShapeDtypeStruct(x.shape, x.dtype),
        scratch_shapes=[pltpu.VMEM((n_slabs, ext + CONV_TAIL, LANES), F32),
                        pltpu.VMEM((n_slabs, ext // PACK, LANES), jnp.uint32),
                        pltpu.VMEM((n_slabs, ext // PACK, LANES), jnp.uint32),
                        pltpu.VMEM((ts, d), F32)],
        compiler_params=_SEQ_PARAMS,
        name="conformer_conv",
    )(x, mod, pre_g.reshape(1, d), post_g.reshape(1, d),
      w1.astype(BF16), b1.reshape(1, 2 * d), taps, bdw.reshape(1, d),
      lng.reshape(1, d), lnb.reshape(1, d), w2.astype(BF16), b2.reshape(1, d))


FFN_CHUNK = 256
FFN_TAIL_CHUNKS = 5
FFN_RING = 2


def _ffn_kernel(x_ref, mod_ref, pre_g_ref, post_g_ref, wup_ref, wdw_ref,
                wdn_ref, o_ref, ubuf, carry, abuf):
    s = pl.program_id(1)
    ts = x_ref.shape[1]
    f = wdn_ref.shape[0]
    kw = wdw_ref.shape[0]
    fc = FFN_CHUNK
    slabs = fc // LANES
    n_chunks = f // fc

    @pl.when(s == 0)
    def _():
        carry[...] = jnp.zeros(carry.shape, F32)

    m = mod_ref[0]
    sh, sc, gt = m[3:4], m[4:5], m[5:6]
    in_gain = pre_g_ref[...] * (1.0 + sc)
    out_gain = gt * post_g_ref[...]
    x = x_ref[0]
    h = (x * _rms_scale(x) * in_gain + sh).astype(BF16)

    def up(j):
        for part, off in enumerate((j * fc, f + j * fc)):
            u = jnp.dot(h, wup_ref[:, off:off + fc], preferred_element_type=F32)
            for c in range(slabs):
                ubuf[j % FFN_RING, part, c, FFN_HALO:FFN_HALO + ts, :] = u[:, _lane_block(c)]

    def conv(j, part, c):
        slab = (part * f + j * fc) // LANES + c
        cols = _lane_block(slab)
        buf = ubuf.at[j % FFN_RING, part, c]
        buf[0:FFN_HALO, :] = carry[slab]
        out = None
        for k in range(kw):
            lo = FFN_HALO - (kw - 1) + k
            term = wdw_ref[k:k + 1, cols] * buf[_rows(lo, ts), :]
            out = term if out is None else out + term
        carry[slab] = buf[ts:ts + FFN_HALO, :]
        return out

    def act(j):
        for c in range(slabs):
            gate = conv(j, 0, c)
            val = conv(j, 1, c)
            cols = _lane_block(j * slabs + c)
            abuf[:, cols] = (gate * jax.nn.sigmoid(gate) * val).astype(BF16)

    up(0)
    for j in range(n_chunks):
        if j + 1 < n_chunks:
            up(j + 1)
        act(j)
    split = (n_chunks - FFN_TAIL_CHUNKS) * fc
    o_ref[0] = (jnp.dot(abuf[:, :split], wdn_ref[:split, :], preferred_element_type=F32)
                + jnp.dot(abuf[:, split:], wdn_ref[split:, :], preferred_element_type=F32))
    y = o_ref[0]
    o_ref[0] = x_ref[0] + y * _rms_scale(y) * out_gain


def _ffn_layer(x, mod, pre_g, post_g, w_up, w_dw, w_down, *, ts):
    bsz, seq, d = x.shape
    f = w_down.shape[0]
    kw = w_dw.shape[0]
    return pl.pallas_call(
        _ffn_kernel,
        grid=(bsz, seq // ts),
        in_specs=[
            _row_tile(ts, d), _mod_spec(d), _resident((1, d)), _resident((1, d)),
            _resident((d, 2 * f)), _resident((kw, 2 * f)), _resident((f, d)),
        ],
        out_specs=_row_tile(ts, d),
        out_shape=jax.ShapeDtypeStruct(x.shape, x.dtype),
        scratch_shapes=[
            pltpu.VMEM((FFN_RING, 2, FFN_CHUNK // LANES, ts + FFN_HALO, LANES), F32),
            pltpu.VMEM((2 * f // LANES, FFN_HALO, LANES), F32),
            pltpu.VMEM((ts, f), BF16)],
        compiler_params=_SEQ_PARAMS,
        name="conv_ffn",
    )(x, mod, pre_g.reshape(1, d), post_g.reshape(1, d),
      w_up.astype(BF16), w_dw, w_down.astype(BF16))


def kernel(x, c, ada_w, ada_b, pre_g, post_g, pool_w, pool_scale, cv_w_pw1, cv_b_pw1, cv_w_dw, cv_b_dw, cv_ln_g, cv_ln_b, cv_w_pw2, cv_b_pw2, ffn_w_up, ffn_w_dw, ffn_w_down):
    depth = ada_w.shape[0]
    mod = _modulation(c, ada_w, ada_b)
    for i in range(depth):
        j = i // 2
        if i % 2 == 0:
            x = _pool_layer(x, mod[i], pre_g[i, 0], post_g[i, 0],
                            pool_w[j], pool_scale[j], ts=512)
        else:
            x = _conf_layer(x, mod[i], pre_g[i, 0], post_g[i, 0],
                            cv_w_pw1[j], cv_b_pw1[j], cv_w_dw[j], cv_b_dw[j],
                            cv_ln_g[j], cv_ln_b[j], cv_w_pw2[j], cv_b_pw2[j],
                            ts=512)
        x = _ffn_layer(x, mod[i], pre_g[i, 1], post_g[i, 1],
                       ffn_w_up[i], ffn_w_dw[i], ffn_w_down[i], ts=1024)
    return x
```

```python
import jax
import jax.numpy as jnp
from jax import lax
from jax.experimental import pallas as pl
from jax.experimental.pallas import tpu as pltpu

F32 = jnp.float32
BF16 = jnp.bfloat16

EPS = 1e-6
POOL_WINDOWS = (2, 4, 8, 16)
N_MOD = 6
SUBLANES = 8
LANES = 128
MXU_COLS = 256
VMEM_LIMIT_BYTES = 60 * 1024 * 1024

POOL_HALO = 32
CONV_HALO = 32
FFN_HALO = 8


def _rms_scale(x):
    return lax.rsqrt(jnp.mean(x * x, axis=-1, keepdims=True) + EPS)


def _rows(start, size):
    return pl.ds(start, size, stride=1)


def _lane_block(m, width=LANES):
    return slice(m * width, (m + 1) * width)


def _resident(shape):
    nd = len(shape)
    return pl.BlockSpec(shape, lambda b, s: (0,) * nd, pipeline_mode=pl.Buffered(1))


def _row_tile(ts, d):
    return pl.BlockSpec((1, ts, d), lambda b, s: (b, s, 0))


def _mod_spec(d):
    return pl.BlockSpec((1, N_MOD, d), lambda b, s: (b, 0, 0))


_SEQ_PARAMS = pltpu.CompilerParams(
    dimension_semantics=("arbitrary", "arbitrary"),
    vmem_limit_bytes=VMEM_LIMIT_BYTES)


def _mod_kernel(c_ref, w_ref, b_ref, o_ref):
    c = c_ref[...]
    ca = c * jax.nn.sigmoid(c)
    o_ref[0] = jnp.dot(ca.astype(BF16), w_ref[0].astype(BF16),
                       preferred_element_type=F32) + b_ref[0]


def _modulation(c, ada_w, ada_b):
    depth, d, nd = ada_w.shape
    bsz = c.shape[0]
    out = pl.pallas_call(
        _mod_kernel,
        grid=(depth, nd // d),
        in_specs=[
            pl.BlockSpec((bsz, d), lambda i, j: (0, 0)),
            pl.BlockSpec((1, d, d), lambda i, j: (i, 0, j)),
            pl.BlockSpec((1, 1, d), lambda i, j: (i, 0, j)),
        ],
        out_specs=pl.BlockSpec((1, bsz, d), lambda i, j: (i, 0, j)),
        out_shape=jax.ShapeDtypeStruct((depth, bsz, nd), F32),
        name="adaln_mod",
    )(c, ada_w, ada_b.reshape(depth, 1, nd))
    return out.reshape(depth, bsz, N_MOD, d)


def _pool_kernel(x_ref, mod_ref, pre_g_ref, post_g_ref, w_ref, scale_ref,
                 o_ref, hbuf, wbuf):
    s = pl.program_id(1)
    ts, d = x_ref.shape[1], x_ref.shape[2]
    gd = w_ref.shape[1]
    slabs_per_group = gd // LANES
    ext = POOL_HALO + ts

    @pl.when(s == 0)
    def _():
        hbuf[:, 0:POOL_HALO, :] = jnp.zeros((d // LANES, POOL_HALO, LANES), F32)

    x = x_ref[0]
    m = mod_ref[0]
    sh, sc, gt = m[0:1], m[1:2], m[2:3]
    h = x * _rms_scale(x) * (pre_g_ref[...] * (1.0 + sc)) + sh
    for j in range(d // LANES):
        hbuf[j, POOL_HALO:ext, :] = h[:, _lane_block(j)]
    pos = lax.broadcasted_iota(jnp.int32, (ts, LANES), 0) + (s * ts + 1)
    ys = []
    for g, w in enumerate(POOL_WINDOWS):
        inv_cnt = 1.0 / jnp.minimum(pos, w).astype(F32)
        pooled = []
        for j in range(g * slabs_per_group, (g + 1) * slabs_per_group):
            src, shift, lo = hbuf, 1, SUBLANES
            while 2 * shift < w:
                wbuf[j, lo:ext, :] = (src[j, lo:ext, :]
                                      + src[j, _rows(lo - shift, ext - lo), :])
                src, shift, lo = wbuf, 2 * shift, lo + SUBLANES
            acc = (src[j, POOL_HALO:ext, :]
                   + src[j, _rows(POOL_HALO - shift, ts), :])
            pooled.append((acc * inv_cnt - hbuf[j, POOL_HALO:ext, :]).astype(BF16))
        ys.append(jnp.dot(jnp.concatenate(pooled, axis=-1), w_ref[g],
                          preferred_element_type=F32))
    y = jnp.concatenate(ys, axis=-1) * scale_ref[...]
    o_ref[0] = x_ref[0] + y * _rms_scale(y) * (gt * post_g_ref[...])
    for j in range(d // LANES):
        hbuf[j, 0:POOL_HALO, :] = hbuf[j, ts:ext, :]


def _pool_layer(x, mod, pre_g, post_g, w_groups, scale, *, ts):
    bsz, seq, d = x.shape
    g, gd, _ = w_groups.shape
    return pl.pallas_call(
        _pool_kernel,
        grid=(bsz, seq // ts),
        in_specs=[
            _row_tile(ts, d), _mod_spec(d), _resident((1, d)), _resident((1, d)),
            _resident((g, gd, gd)), _resident((1, d)),
        ],
        out_specs=_row_tile(ts, d),
        out_shape=jax.ShapeDtypeStruct(x.shape, x.dtype),
        scratch_shapes=[pltpu.VMEM((d // LANES, ts + POOL_HALO, LANES), F32)] * 2,
        compiler_params=_SEQ_PARAMS,
        name="pool_mixer",
    )(x, mod, pre_g.reshape(1, d), post_g.reshape(1, d),
      w_groups.astype(BF16), scale.reshape(1, d))


CONV_GROUPS = 4
PACK = 2
GROUP_ROWS = PACK * SUBLANES
CONV_TAIL = SUBLANES


def _conf_kernel(x_ref, mod_ref, pre_g_ref, post_g_ref, w1_ref, b1_ref,
                 taps_ref, bdw_ref, lng_ref, lnb_ref, w2_ref, b2_ref,
                 o_ref, ubuf, even, odd, cbuf):
    s = pl.program_id(1)
    ts, d = x_ref.shape[1], x_ref.shape[2]
    kw = taps_ref.shape[1] // GROUP_ROWS
    n_slabs = d // LANES
    ext = CONV_HALO + ts
    block_rows = CONV_GROUPS * GROUP_ROWS

    @pl.when(s == 0)
    def _():
        ubuf[:, 0:CONV_HALO, :] = jnp.zeros((n_slabs, CONV_HALO, LANES), F32)
        ubuf[:, ext:ext + CONV_TAIL, :] = jnp.zeros((n_slabs, CONV_TAIL, LANES), F32)

    x = x_ref[0]
    m = mod_ref[0]
    sh, sc, gt = m[0:1], m[1:2], m[2:3]
    h = (x * _rms_scale(x) * (pre_g_ref[...] * (1.0 + sc)) + sh).astype(BF16)
    for j in range(d // MXU_COLS):
        vcols = _lane_block(j, MXU_COLS)
        gcols = slice(d + j * MXU_COLS, d + (j + 1) * MXU_COLS)
        val = jnp.dot(h, w1_ref[:, vcols], preferred_element_type=F32) + b1_ref[:, vcols]
        gate = jnp.dot(h, w1_ref[:, gcols], preferred_element_type=F32) + b1_ref[:, gcols]
        u = val * jax.nn.sigmoid(gate)
        for i in range(MXU_COLS // LANES):
            ubuf[j * (MXU_COLS // LANES) + i, CONV_HALO:ext, :] = u[:, _lane_block(i)]

    base = CONV_HALO - (kw - 1)
    for j in range(n_slabs):
        cols = _lane_block(j)
        even[j] = pltpu.bitcast(ubuf[j, 0:ext, :].astype(BF16), jnp.uint32)
        odd[j] = pltpu.bitcast(ubuf[j, _rows(1, ext), :].astype(BF16), jnp.uint32)
        bias = jnp.broadcast_to(bdw_ref[:, cols], (GROUP_ROWS, LANES))

        def body(i, carry, j=j, cols=cols, bias=bias):
            r0 = pl.multiple_of(i * block_rows, block_rows)
            w0 = pl.multiple_of(i * (block_rows // PACK), block_rows // PACK)
            accs = [None] * CONV_GROUPS
            windows = {}

            def window(first):
                if first not in windows:
                    src = even if first % PACK == 0 else odd
                    win = src[j, _rows(w0 + first // PACK, SUBLANES), :]
                    windows[first] = pltpu.bitcast(win, BF16)
                return windows[first]

            for k in range(kw):
                tap = taps_ref[j, k * GROUP_ROWS:(k + 1) * GROUP_ROWS, :]
                for g in range(CONV_GROUPS):
                    win = window(base + g * GROUP_ROWS + k)
                    prod = win.astype(F32) * tap.astype(F32)
                    accs[g] = prod if accs[g] is None else accs[g] + prod
            for g in range(CONV_GROUPS):
                cbuf[pl.ds(r0 + g * GROUP_ROWS, GROUP_ROWS), cols] = accs[g] + bias
            return carry

        lax.fori_loop(0, ts // block_rows, body, 0)

    v = cbuf[...]
    mu = jnp.mean(v, axis=-1, keepdims=True)
    vc = v - mu
    var = jnp.mean(vc * vc, axis=-1, keepdims=True)
    ln = vc * lax.rsqrt(var + EPS) * lng_ref[...] + lnb_ref[...]
    act = (ln * jax.nn.sigmoid(ln)).astype(BF16)
    y = jnp.dot(act, w2_ref[...], preferred_element_type=F32) + b2_ref[...]
    o_ref[0] = x_ref[0] + y * _rms_scale(y) * (gt * post_g_ref[...])
    for j in range(n_slabs):
        ubuf[j, 0:CONV_HALO, :] = ubuf[j, ts:ext, :]


def _conf_layer(x, mod, pre_g, post_g, w1, b1, wdw, bdw, lng, lnb, w2, b2, *, ts):
    bsz, seq, d = x.shape
    kw = wdw.shape[0]
    n_slabs = d // LANES
    ext = CONV_HALO + ts
    taps = wdw.astype(BF16).reshape(kw, n_slabs, 1, LANES).transpose(1, 0, 2, 3)
    taps = jnp.broadcast_to(taps, (n_slabs, kw, GROUP_ROWS, LANES))
    taps = taps.reshape(n_slabs, kw * GROUP_ROWS, LANES)
    return pl.pallas_call(
        _conf_kernel,
        grid=(bsz, seq // ts),
        in_specs=[
            _row_tile(ts, d), _mod_spec(d), _resident((1, d)), _resident((1, d)),
            _resident((d, 2 * d)), _resident((1, 2 * d)),
            _resident(taps.shape), _resident((1, d)), _resident((1, d)),
            _resident((1, d)), _resident((d, d)), _resident((1, d)),
        ],
        out_specs=_row_tile(ts, d),
        out_shape=jax.ShapeDtypeStruct(x.shape, x.dtype),
        scratch_shapes=[pltpu.VMEM((n_slabs, ext + CONV_TAIL, LANES), F32),
                        pltpu.VMEM((n_slabs, ext // PACK, LANES), jnp.uint32),
                        pltpu.VMEM((n_slabs, ext // PACK, LANES), jnp.uint32),
                        pltpu.VMEM((ts, d), F32)],
        compiler_params=_SEQ_PARAMS,
        name="conformer_conv",
    )(x, mod, pre_g.reshape(1, d), post_g.reshape(1, d),
      w1.astype(BF16), b1.reshape(1, 2 * d), taps, bdw.reshape(1, d),
      lng.reshape(1, d), lnb.reshape(1, d), w2.astype(BF16), b2.reshape(1, d))


FFN_CHUNK = 256
FFN_TAIL_CHUNKS = 5
FFN_RING = 2


def _ffn_kernel(x_ref, mod_ref, pre_g_ref, post_g_ref, wup_ref, wdw_ref,
                wdn_ref, o_ref, ubuf, carry, abuf):
    s = pl.program_id(1)
    ts = x_ref.shape[1]
    f = wdn_ref.shape[0]
    kw = wdw_ref.shape[0]
    fc = FFN_CHUNK
    slabs = fc // LANES
    n_chunks = f // fc

    @pl.when(s == 0)
    def _():
        carry[...] = jnp.zeros(carry.shape, F32)

    m = mod_ref[0]
    sh, sc, gt = m[3:4], m[4:5], m[5:6]
    in_gain = pre_g_ref[...] * (1.0 + sc)
    out_gain = gt * post_g_ref[...]
    x = x_ref[0]
    h = (x * _rms_scale(x) * in_gain + sh).astype(BF16)

    def up(j):
        for part, off in enumerate((j * fc, f + j * fc)):
            u = jnp.dot(h, wup_ref[:, off:off + fc], preferred_element_type=F32)
            for c in range(slabs):
                ubuf[j % FFN_RING, part, c, FFN_HALO:FFN_HALO + ts, :] = u[:, _lane_block(c)]

    def conv(j, part, c):
        slab = (part * f + j * fc) // LANES + c
        cols = _lane_block(slab)
        buf = ubuf.at[j % FFN_RING, part, c]
        buf[0:FFN_HALO, :] = carry[slab]
        out = None
        for k in range(kw):
            lo = FFN_HALO - (kw - 1) + k
            term = wdw_ref[k:k + 1, cols] * buf[_rows(lo, ts), :]
            out = term if out is None else out + term
        carry[slab] = buf[ts:ts + FFN_HALO, :]
        return out

    def act(j):
        for c in range(slabs):
            gate = conv(j, 0, c)
            val = conv(j, 1, c)
            cols = _lane_block(j * slabs + c)
            abuf[:, cols] = (gate * jax.nn.sigmoid(gate) * val).astype(BF16)

    up(0)
    for j in range(n_chunks):
        if j + 1 < n_chunks:
            up(j + 1)
        act(j)
    split = (n_chunks - FFN_TAIL_CHUNKS) * fc
    o_ref[0] = (jnp.dot(abuf[:, :split], wdn_ref[:split, :], preferred_element_type=F32)
                + jnp.dot(abuf[:, split:], wdn_ref[split:, :], preferred_element_type=F32))
    y = o_ref[0]
    o_ref[0] = x_ref[0] + y * _rms_scale(y) * out_gain


def _ffn_layer(x, mod, pre_g, post_g, w_up, w_dw, w_down, *, ts):
    bsz, seq, d = x.shape
    f = w_down.shape[0]
    kw = w_dw.shape[0]
    return pl.pallas_call(
        _ffn_kernel,
        grid=(bsz, seq // ts),
        in_specs=[
            _row_tile(ts, d), _mod_spec(d), _resident((1, d)), _resident((1, d)),
            _resident((d, 2 * f)), _resident((kw, 2 * f)), _resident((f, d)),
        ],
        out_specs=_row_tile(ts, d),
        out_shape=jax.ShapeDtypeStruct(x.shape, x.dtype),
        scratch_shapes=[
            pltpu.VMEM((FFN_RING, 2, FFN_CHUNK // LANES, ts + FFN_HALO, LANES), F32),
            pltpu.VMEM((2 * f // LANES, FFN_HALO, LANES), F32),
            pltpu.VMEM((ts, f), BF16)],
        compiler_params=_SEQ_PARAMS,
        name="conv_ffn",
    )(x, mod, pre_g.reshape(1, d), post_g.reshape(1, d),
      w_up.astype(BF16), w_dw, w_down.astype(BF16))


def kernel(x, c, ada_w, ada_b, pre_g, post_g, pool_w, pool_scale, cv_w_pw1, cv_b_pw1, cv_w_dw, cv_b_dw, cv_ln_g, cv_ln_b, cv_w_pw2, cv_b_pw2, ffn_w_up, ffn_w_dw, ffn_w_down):
    depth = ada_w.shape[0]
    mod = _modulation(c, ada_w, ada_b)
    for i in range(depth):
        j = i // 2
        if i % 2 == 0:
            x = _pool_layer(x, mod[i], pre_g[i, 0], post_g[i, 0],
                            pool_w[j], pool_scale[j], ts=1024)
        else:
            x = _conf_layer(x, mod[i], pre_g[i, 0], post_g[i, 0],
                            cv_w_pw1[j], cv_b_pw1[j], cv_w_dw[j], cv_b_dw[j],
                            cv_ln_g[j], cv_ln_b[j], cv_w_pw2[j], cv_b_pw2[j],
                            ts=1024)
        x = _ffn_layer(x, mod[i], pre_g[i, 1], post_g[i, 1],
                       ffn_w_up[i], ffn_w_dw[i], ffn_w_down[i], ts=1024)
    return x
```

```python
import jax
import jax.numpy as jnp
from jax import lax
from jax.experimental import pallas as pl
from jax.experimental.pallas import tpu as pltpu

F32 = jnp.float32
BF16 = jnp.bfloat16

EPS = 1e-6
POOL_WINDOWS = (2, 4, 8, 16)
N_MOD = 6
SUBLANES = 8
LANES = 128
MXU_COLS = 256
VMEM_LIMIT_BYTES = 60 * 1024 * 1024
ROW_TILE = 1024

POOL_HALO = 32
CONV_HALO = 32
FFN_HALO = 8


def _rms_scale(x):
    return lax.rsqrt(jnp.mean(x * x, axis=-1, keepdims=True) + EPS)


def _rows(start, size):
    return pl.ds(start, size, stride=1)


def _lane_block(m, width=LANES):
    return slice(m * width, (m + 1) * width)


def _resident(shape):
    nd = len(shape)
    return pl.BlockSpec(shape, lambda b, s: (0,) * nd, pipeline_mode=pl.Buffered(1))


def _row_tile(ts, d):
    return pl.BlockSpec((1, ts, d), lambda b, s: (b, s, 0))


def _mod_spec(d):
    return pl.BlockSpec((1, N_MOD, d), lambda b, s: (b, 0, 0))


_SEQ_PARAMS = pltpu.CompilerParams(
    dimension_semantics=("arbitrary", "arbitrary"),
    vmem_limit_bytes=VMEM_LIMIT_BYTES)


def _mod_kernel(c_ref, w_ref, b_ref, o_ref):
    c = c_ref[...]
    ca = c * jax.nn.sigmoid(c)
    o_ref[0] = jnp.dot(ca.astype(BF16), w_ref[0].astype(BF16),
                       preferred_element_type=F32) + b_ref[0]


def _modulation(c, ada_w, ada_b):
    depth, d, nd = ada_w.shape
    bsz = c.shape[0]
    out = pl.pallas_call(
        _mod_kernel,
        grid=(depth, nd // d),
        in_specs=[
            pl.BlockSpec((bsz, d), lambda i, j: (0, 0)),
            pl.BlockSpec((1, d, d), lambda i, j: (i, 0, j)),
            pl.BlockSpec((1, 1, d), lambda i, j: (i, 0, j)),
        ],
        out_specs=pl.BlockSpec((1, bsz, d), lambda i, j: (i, 0, j)),
        out_shape=jax.ShapeDtypeStruct((depth, bsz, nd), F32),
        name="adaln_mod",
    )(c, ada_w, ada_b.reshape(depth, 1, nd))
    return out.reshape(depth, bsz, N_MOD, d)


def _pool_kernel(x_ref, mod_ref, pre_g_ref, post_g_ref, w_ref, scale_ref,
                 o_ref, hbuf, wbuf):
    s = pl.program_id(1)
    ts, d = x_ref.shape[1], x_ref.shape[2]
    gd = w_ref.shape[1]
    slabs_per_group = gd // LANES
    ext = POOL_HALO + ts

    @pl.when(s == 0)
    def _():
        hbuf[:, 0:POOL_HALO, :] = jnp.zeros((d // LANES, POOL_HALO, LANES), F32)

    x = x_ref[0]
    m = mod_ref[0]
    sh, sc, gt = m[0:1], m[1:2], m[2:3]
    h = x * _rms_scale(x) * (pre_g_ref[...] * (1.0 + sc)) + sh
    for j in range(d // LANES):
        hbuf[j, POOL_HALO:ext, :] = h[:, _lane_block(j)]
    pos = lax.broadcasted_iota(jnp.int32, (ts, LANES), 0) + (s * ts + 1)
    ys = []
    for g, w in enumerate(POOL_WINDOWS):
        inv_cnt = 1.0 / jnp.minimum(pos, w).astype(F32)
        pooled = []
        for j in range(g * slabs_per_group, (g + 1) * slabs_per_group):
            src, shift, lo = hbuf, 1, SUBLANES
            while 2 * shift < w:
                wbuf[j, lo:ext, :] = (src[j, lo:ext, :]
                                      + src[j, _rows(lo - shift, ext - lo), :])
                src, shift, lo = wbuf, 2 * shift, lo + SUBLANES
            acc = (src[j, POOL_HALO:ext, :]
                   + src[j, _rows(POOL_HALO - shift, ts), :])
            pooled.append((acc * inv_cnt - hbuf[j, POOL_HALO:ext, :]).astype(BF16))
        ys.append(jnp.dot(jnp.concatenate(pooled, axis=-1), w_ref[g],
                          preferred_element_type=F32))
    y = jnp.concatenate(ys, axis=-1) * scale_ref[...]
    o_ref[0] = x_ref[0] + y * _rms_scale(y) * (gt * post_g_ref[...])
    for j in range(d // LANES):
        hbuf[j, 0:POOL_HALO, :] = hbuf[j, ts:ext, :]


def _pool_layer(x, mod, pre_g, post_g, w_groups, scale, *, ts):
    bsz, seq, d = x.shape
    g, gd, _ = w_groups.shape
    assert seq % ts == 0 and gd % LANES == 0 and g == len(POOL_WINDOWS)
    assert all(w & (w - 1) == 0 for w in POOL_WINDOWS)
    assert max(POOL_WINDOWS).bit_length() - 1 <= POOL_HALO // SUBLANES
    return pl.pallas_call(
        _pool_kernel,
        grid=(bsz, seq // ts),
        in_specs=[
            _row_tile(ts, d), _mod_spec(d), _resident((1, d)), _resident((1, d)),
            _resident((g, gd, gd)), _resident((1, d)),
        ],
        out_specs=_row_tile(ts, d),
        out_shape=jax.ShapeDtypeStruct(x.shape, x.dtype),
        scratch_shapes=[pltpu.VMEM((d // LANES, ts + POOL_HALO, LANES), F32)] * 2,
        compiler_params=_SEQ_PARAMS,
        name="pool_mixer",
    )(x, mod, pre_g.reshape(1, d), post_g.reshape(1, d),
      w_groups.astype(BF16), scale.reshape(1, d))


CONV_GROUPS = 4
PACK = 2
GROUP_ROWS = PACK * SUBLANES
CONV_TAIL = SUBLANES


def _conf_kernel(x_ref, mod_ref, pre_g_ref, post_g_ref, w1_ref, b1_ref,
                 taps_ref, bdw_ref, lng_ref, lnb_ref, w2_ref, b2_ref,
                 o_ref, ubuf, even, odd, cbuf):
    s = pl.program_id(1)
    ts, d = x_ref.shape[1], x_ref.shape[2]
    kw = taps_ref.shape[1] // GROUP_ROWS
    n_slabs = d // LANES
    ext = CONV_HALO + ts
    block_rows = CONV_GROUPS * GROUP_ROWS

    @pl.when(s == 0)
    def _():
        ubuf[:, 0:CONV_HALO, :] = jnp.zeros((n_slabs, CONV_HALO, LANES), F32)
        ubuf[:, ext:ext + CONV_TAIL, :] = jnp.zeros((n_slabs, CONV_TAIL, LANES), F32)

    x = x_ref[0]
    m = mod_ref[0]
    sh, sc, gt = m[0:1], m[1:2], m[2:3]
    h = (x * _rms_scale(x) * (pre_g_ref[...] * (1.0 + sc)) + sh).astype(BF16)
    for j in range(d // MXU_COLS):
        vcols = _lane_block(j, MXU_COLS)
        gcols = slice(d + j * MXU_COLS, d + (j + 1) * MXU_COLS)
        val = jnp.dot(h, w1_ref[:, vcols], preferred_element_type=F32) + b1_ref[:, vcols]
        gate = jnp.dot(h, w1_ref[:, gcols], preferred_element_type=F32) + b1_ref[:, gcols]
        u = val * jax.nn.sigmoid(gate)
        for i in range(MXU_COLS // LANES):
            ubuf[j * (MXU_COLS // LANES) + i, CONV_HALO:ext, :] = u[:, _lane_block(i)]

    base = CONV_HALO - (kw - 1)
    for j in range(n_slabs):
        cols = _lane_block(j)
        even[j] = pltpu.bitcast(ubuf[j, 0:ext, :].astype(BF16), jnp.uint32)
        odd[j] = pltpu.bitcast(ubuf[j, _rows(1, ext), :].astype(BF16), jnp.uint32)
        bias = jnp.broadcast_to(bdw_ref[:, cols], (GROUP_ROWS, LANES))

        def body(i, carry, j=j, cols=cols, bias=bias):
            r0 = pl.multiple_of(i * block_rows, block_rows)
            w0 = pl.multiple_of(i * (block_rows // PACK), block_rows // PACK)
            accs = [None] * CONV_GROUPS
            windows = {}

            def window(first):
                if first not in windows:
                    src = even if first % PACK == 0 else odd
                    win = src[j, _rows(w0 + first // PACK, SUBLANES), :]
                    windows[first] = pltpu.bitcast(win, BF16)
                return windows[first]

            for k in range(kw):
                tap = taps_ref[j, k * GROUP_ROWS:(k + 1) * GROUP_ROWS, :]
                for g in range(CONV_GROUPS):
                    win = window(base + g * GROUP_ROWS + k)
                    prod = win.astype(F32) * tap.astype(F32)
                    accs[g] = prod if accs[g] is None else accs[g] + prod
            for g in range(CONV_GROUPS):
                cbuf[pl.ds(r0 + g * GROUP_ROWS, GROUP_ROWS), cols] = accs[g] + bias
            return carry

        lax.fori_loop(0, ts // block_rows, body, 0)

    v = cbuf[...]
    mu = jnp.mean(v, axis=-1, keepdims=True)
    vc = v - mu
    var = jnp.mean(vc * vc, axis=-1, keepdims=True)
    ln = vc * lax.rsqrt(var + EPS) * lng_ref[...] + lnb_ref[...]
    act = (ln * jax.nn.sigmoid(ln)).astype(BF16)
    y = jnp.dot(act, w2_ref[...], preferred_element_type=F32) + b2_ref[...]
    o_ref[0] = x_ref[0] + y * _rms_scale(y) * (gt * post_g_ref[...])
    for j in range(n_slabs):
        ubuf[j, 0:CONV_HALO, :] = ubuf[j, ts:ext, :]


def _conf_layer(x, mod, pre_g, post_g, w1, b1, wdw, bdw, lng, lnb, w2, b2, *, ts):
    bsz, seq, d = x.shape
    kw = wdw.shape[0]
    n_slabs = d // LANES
    ext = CONV_HALO + ts
    assert seq % ts == 0 and ts % (CONV_GROUPS * GROUP_ROWS) == 0
    assert d % MXU_COLS == 0 and kw - 1 <= CONV_HALO
    taps = wdw.astype(BF16).reshape(kw, n_slabs, 1, LANES).transpose(1, 0, 2, 3)
    taps = jnp.broadcast_to(taps, (n_slabs, kw, GROUP_ROWS, LANES))
    taps = taps.reshape(n_slabs, kw * GROUP_ROWS, LANES)
    return pl.pallas_call(
        _conf_kernel,
        grid=(bsz, seq // ts),
        in_specs=[
            _row_tile(ts, d), _mod_spec(d), _resident((1, d)), _resident((1, d)),
            _resident((d, 2 * d)), _resident((1, 2 * d)),
            _resident(taps.shape), _resident((1, d)), _resident((1, d)),
            _resident((1, d)), _resident((d, d)), _resident((1, d)),
        ],
        out_specs=_row_tile(ts, d),
        out_shape=jax.ShapeDtypeStruct(x.shape, x.dtype),
        scratch_shapes=[pltpu.VMEM((n_slabs, ext + CONV_TAIL, LANES), F32),
                        pltpu.VMEM((n_slabs, ext // PACK, LANES), jnp.uint32),
                        pltpu.VMEM((n_slabs, ext // PACK, LANES), jnp.uint32),
                        pltpu.VMEM((ts, d), F32)],
        compiler_params=_SEQ_PARAMS,
        name="conformer_conv",
    )(x, mod, pre_g.reshape(1, d), post_g.reshape(1, d),
      w1.astype(BF16), b1.reshape(1, 2 * d), taps, bdw.reshape(1, d),
      lng.reshape(1, d), lnb.reshape(1, d), w2.astype(BF16), b2.reshape(1, d))


FFN_CHUNK = 256
FFN_TAIL_CHUNKS = 5
FFN_RING = 2


def _ffn_kernel(x_ref, mod_ref, pre_g_ref, post_g_ref, wup_ref, wdw_ref,
                wdn_ref, o_ref, ubuf, carry, abuf):
    s = pl.program_id(1)
    ts = x_ref.shape[1]
    f = wdn_ref.shape[0]
    kw = wdw_ref.shape[0]
    fc = FFN_CHUNK
    slabs = fc // LANES
    n_chunks = f // fc

    @pl.when(s == 0)
    def _():
        carry[...] = jnp.zeros(carry.shape, F32)

    m = mod_ref[0]
    sh, sc, gt = m[3:4], m[4:5], m[5:6]
    in_gain = pre_g_ref[...] * (1.0 + sc)
    out_gain = gt * post_g_ref[...]
    x = x_ref[0]
    h = (x * _rms_scale(x) * in_gain + sh).astype(BF16)

    def up(j):
        for part, off in enumerate((j * fc, f + j * fc)):
            u = jnp.dot(h, wup_ref[:, off:off + fc], preferred_element_type=F32)
            for c in range(slabs):
                ubuf[j % FFN_RING, part, c, FFN_HALO:FFN_HALO + ts, :] = u[:, _lane_block(c)]

    def conv(j, part, c):
        slab = (part * f + j * fc) // LANES + c
        cols = _lane_block(slab)
        buf = ubuf.at[j % FFN_RING, part, c]
        buf[0:FFN_HALO, :] = carry[slab]
        out = None
        for k in range(kw):
            lo = FFN_HALO - (kw - 1) + k
            term = wdw_ref[k:k + 1, cols] * buf[_rows(lo, ts), :]
            out = term if out is None else out + term
        carry[slab] = buf[ts:ts + FFN_HALO, :]
        return out

    def act(j):
        for c in range(slabs):
            gate = conv(j, 0, c)
            val = conv(j, 1, c)
            cols = _lane_block(j * slabs + c)
            abuf[:, cols] = (gate * jax.nn.sigmoid(gate) * val).astype(BF16)

    up(0)
    for j in range(n_chunks):
        if j + 1 < n_chunks:
            up(j + 1)
        act(j)
    split = (n_chunks - FFN_TAIL_CHUNKS) * fc
    o_ref[0] = (jnp.dot(abuf[:, :split], wdn_ref[:split, :], preferred_element_type=F32)
                + jnp.dot(abuf[:, split:], wdn_ref[split:, :], preferred_element_type=F32))
    y = o_ref[0]
    o_ref[0] = x_ref[0] + y * _rms_scale(y) * out_gain


def _ffn_layer(x, mod, pre_g, post_g, w_up, w_dw, w_down, *, ts):
    bsz, seq, d = x.shape
    f = w_down.shape[0]
    kw = w_dw.shape[0]
    assert seq % ts == 0 and f % FFN_CHUNK == 0 and kw - 1 <= FFN_HALO
    assert f // FFN_CHUNK > FFN_TAIL_CHUNKS
    return pl.pallas_call(
        _ffn_kernel,
        grid=(bsz, seq // ts),
        in_specs=[
            _row_tile(ts, d), _mod_spec(d), _resident((1, d)), _resident((1, d)),
            _resident((d, 2 * f)), _resident((kw, 2 * f)), _resident((f, d)),
        ],
        out_specs=_row_tile(ts, d),
        out_shape=jax.ShapeDtypeStruct(x.shape, x.dtype),
        scratch_shapes=[
            pltpu.VMEM((FFN_RING, 2, FFN_CHUNK // LANES, ts + FFN_HALO, LANES), F32),
            pltpu.VMEM((2 * f // LANES, FFN_HALO, LANES), F32),
            pltpu.VMEM((ts, f), BF16)],
        compiler_params=_SEQ_PARAMS,
        name="conv_ffn",
    )(x, mod, pre_g.reshape(1, d), post_g.reshape(1, d),
      w_up.astype(BF16), w_dw, w_down.astype(BF16))


def kernel(x, c, ada_w, ada_b, pre_g, post_g, pool_w, pool_scale, cv_w_pw1, cv_b_pw1, cv_w_dw, cv_b_dw, cv_ln_g, cv_ln_b, cv_w_pw2, cv_b_pw2, ffn_w_up, ffn_w_dw, ffn_w_down):
    depth = ada_w.shape[0]
    mod = _modulation(c, ada_w, ada_b)
    for i in range(depth):
        j = i // 2
        if i % 2 == 0:
            x = _pool_layer(x, mod[i], pre_g[i, 0], post_g[i, 0],
                            pool_w[j], pool_scale[j], ts=ROW_TILE)
        else:
            x = _conf_layer(x, mod[i], pre_g[i, 0], post_g[i, 0],
                            cv_w_pw1[j], cv_b_pw1[j], cv_w_dw[j], cv_b_dw[j],
                            cv_ln_g[j], cv_ln_b[j], cv_w_pw2[j], cv_b_pw2[j],
                            ts=ROW_TILE)
        x = _ffn_layer(x, mod[i], pre_g[i, 1], post_g[i, 1],
                       ffn_w_up[i], ffn_w_dw[i], ffn_w_down[i], ts=ROW_TILE)
    return x
```

```python
import jax
import jax.numpy as jnp
from jax import lax
from jax.experimental import pallas as pl
from jax.experimental.pallas import tpu as pltpu

F32 = jnp.float32
BF16 = jnp.bfloat16

EPS = 1e-6
POOL_WINDOWS = (2, 4, 8, 16)
N_MOD = 6
SUBLANES = 8
LANES = 128
MXU_COLS = 256
VMEM_LIMIT_BYTES = 60 * 1024 * 1024
ROW_TILE = 1024

POOL_HALO = 32
CONV_HALO = 32
FFN_HALO = 8


def _rms_scale(x):
    return lax.rsqrt(jnp.mean(x * x, axis=-1, keepdims=True) + EPS)


def _rows(start, size):
    return pl.ds(start, size, stride=1)


def _lane_block(m, width=LANES):
    return slice(m * width, (m + 1) * width)


def _resident(shape):
    nd = len(shape)
    return pl.BlockSpec(shape, lambda b, s: (0,) * nd, pipeline_mode=pl.Buffered(1))


def _row_tile(ts, d):
    return pl.BlockSpec((1, ts, d), lambda b, s: (b, s, 0))


def _mod_spec(d):
    return pl.BlockSpec((1, N_MOD, d), lambda b, s: (b, 0, 0))


def _seq_params(n_operands, cast_operands):
    return pltpu.CompilerParams(
        dimension_semantics=("arbitrary", "arbitrary"),
        vmem_limit_bytes=VMEM_LIMIT_BYTES,
        allow_input_fusion=[i in cast_operands for i in range(n_operands)])


def _mod_kernel(c_ref, w_ref, b_ref, o_ref):
    c = c_ref[...]
    ca = c * jax.nn.sigmoid(c)
    o_ref[0] = jnp.dot(ca.astype(BF16), w_ref[0].astype(BF16),
                       preferred_element_type=F32) + b_ref[0]


def _modulation(c, ada_w, ada_b):
    depth, d, nd = ada_w.shape
    bsz = c.shape[0]
    out = pl.pallas_call(
        _mod_kernel,
        grid=(depth, nd // d),
        in_specs=[
            pl.BlockSpec((bsz, d), lambda i, j: (0, 0)),
            pl.BlockSpec((1, d, d), lambda i, j: (i, 0, j)),
            pl.BlockSpec((1, 1, d), lambda i, j: (i, 0, j)),
        ],
        out_specs=pl.BlockSpec((1, bsz, d), lambda i, j: (i, 0, j)),
        out_shape=jax.ShapeDtypeStruct((depth, bsz, nd), F32),
        name="adaln_mod",
    )(c, ada_w, ada_b.reshape(depth, 1, nd))
    return out.reshape(depth, bsz, N_MOD, d)


def _pool_kernel(x_ref, mod_ref, pre_g_ref, post_g_ref, w_ref, scale_ref,
                 o_ref, hbuf, wbuf):
    s = pl.program_id(1)
    ts, d = x_ref.shape[1], x_ref.shape[2]
    gd = w_ref.shape[1]
    slabs_per_group = gd // LANES
    ext = POOL_HALO + ts

    @pl.when(s == 0)
    def _():
        hbuf[:, 0:POOL_HALO, :] = jnp.zeros((d // LANES, POOL_HALO, LANES), F32)

    x = x_ref[0]
    m = mod_ref[0]
    sh, sc, gt = m[0:1], m[1:2], m[2:3]
    h = x * _rms_scale(x) * (pre_g_ref[...] * (1.0 + sc)) + sh
    for j in range(d // LANES):
        hbuf[j, POOL_HALO:ext, :] = h[:, _lane_block(j)]
    pos = lax.broadcasted_iota(jnp.int32, (ts, LANES), 0) + (s * ts + 1)
    ys = []
    for g, w in enumerate(POOL_WINDOWS):
        inv_cnt = 1.0 / jnp.minimum(pos, w).astype(F32)
        pooled = []
        for j in range(g * slabs_per_group, (g + 1) * slabs_per_group):
            src, shift, lo = hbuf, 1, SUBLANES
            while 2 * shift < w:
                wbuf[j, lo:ext, :] = (src[j, lo:ext, :]
                                      + src[j, _rows(lo - shift, ext - lo), :])
                src, shift, lo = wbuf, 2 * shift, lo + SUBLANES
            acc = (src[j, POOL_HALO:ext, :]
                   + src[j, _rows(POOL_HALO - shift, ts), :])
            pooled.append((acc * inv_cnt - hbuf[j, POOL_HALO:ext, :]).astype(BF16))
        ys.append(jnp.dot(jnp.concatenate(pooled, axis=-1), w_ref[g],
                          preferred_element_type=F32))
    y = jnp.concatenate(ys, axis=-1) * scale_ref[...]
    o_ref[0] = x_ref[0] + y * _rms_scale(y) * (gt * post_g_ref[...])
    for j in range(d // LANES):
        hbuf[j, 0:POOL_HALO, :] = hbuf[j, ts:ext, :]


def _pool_layer(x, mod, pre_g, post_g, w_groups, scale, *, ts):
    bsz, seq, d = x.shape
    g, gd, _ = w_groups.shape
    assert seq % ts == 0 and gd % LANES == 0 and g == len(POOL_WINDOWS)
    assert all(w & (w - 1) == 0 for w in POOL_WINDOWS)
    assert max(POOL_WINDOWS).bit_length() - 1 <= POOL_HALO // SUBLANES
    return pl.pallas_call(
        _pool_kernel,
        grid=(bsz, seq // ts),
        in_specs=[
            _row_tile(ts, d), _mod_spec(d), _resident((1, d)), _resident((1, d)),
            _resident((g, gd, gd)), _resident((1, d)),
        ],
        out_specs=_row_tile(ts, d),
        out_shape=jax.ShapeDtypeStruct(x.shape, x.dtype),
        scratch_shapes=[pltpu.VMEM((d // LANES, ts + POOL_HALO, LANES), F32)] * 2,
        compiler_params=_seq_params(6, (4,)),
        name="pool_mixer",
    )(x, mod, pre_g.reshape(1, d), post_g.reshape(1, d),
      w_groups.astype(BF16), scale.reshape(1, d))


CONV_GROUPS = 4
PACK = 2
GROUP_ROWS = PACK * SUBLANES
CONV_TAIL = SUBLANES


def _conf_kernel(x_ref, mod_ref, pre_g_ref, post_g_ref, w1_ref, b1_ref,
                 taps_ref, bdw_ref, lng_ref, lnb_ref, w2_ref, b2_ref,
                 o_ref, ubuf, even, odd, cbuf):
    s = pl.program_id(1)
    ts, d = x_ref.shape[1], x_ref.shape[2]
    kw = taps_ref.shape[1] // GROUP_ROWS
    n_slabs = d // LANES
    ext = CONV_HALO + ts
    block_rows = CONV_GROUPS * GROUP_ROWS

    @pl.when(s == 0)
    def _():
        ubuf[:, 0:CONV_HALO, :] = jnp.zeros((n_slabs, CONV_HALO, LANES), F32)
        ubuf[:, ext:ext + CONV_TAIL, :] = jnp.zeros((n_slabs, CONV_TAIL, LANES), F32)

    x = x_ref[0]
    m = mod_ref[0]
    sh, sc, gt = m[0:1], m[1:2], m[2:3]
    h = (x * _rms_scale(x) * (pre_g_ref[...] * (1.0 + sc)) + sh).astype(BF16)
    for j in range(d // MXU_COLS):
        vcols = _lane_block(j, MXU_COLS)
        gcols = slice(d + j * MXU_COLS, d + (j + 1) * MXU_COLS)
        val = jnp.dot(h, w1_ref[:, vcols], preferred_element_type=F32) + b1_ref[:, vcols]
        gate = jnp.dot(h, w1_ref[:, gcols], preferred_element_type=F32) + b1_ref[:, gcols]
        u = val * jax.nn.sigmoid(gate)
        for i in range(MXU_COLS // LANES):
            ubuf[j * (MXU_COLS // LANES) + i, CONV_HALO:ext, :] = u[:, _lane_block(i)]

    base = CONV_HALO - (kw - 1)
    for j in range(n_slabs):
        cols = _lane_block(j)
        even[j] = pltpu.bitcast(ubuf[j, 0:ext, :].astype(BF16), jnp.uint32)
        odd[j] = pltpu.bitcast(ubuf[j, _rows(1, ext), :].astype(BF16), jnp.uint32)
        bias = jnp.broadcast_to(bdw_ref[:, cols], (GROUP_ROWS, LANES))

        def body(i, carry, j=j, cols=cols, bias=bias):
            r0 = pl.multiple_of(i * block_rows, block_rows)
            w0 = pl.multiple_of(i * (block_rows // PACK), block_rows // PACK)
            accs = [None] * CONV_GROUPS
            windows = {}

            def window(first):
                if first not in windows:
                    src = even if first % PACK == 0 else odd
                    win = src[j, _rows(w0 + first // PACK, SUBLANES), :]
                    windows[first] = pltpu.bitcast(win, BF16)
                return windows[first]

            for k in range(kw):
                tap = taps_ref[j, k * GROUP_ROWS:(k + 1) * GROUP_ROWS, :]
                for g in range(CONV_GROUPS):
                    win = window(base + g * GROUP_ROWS + k)
                    prod = win.astype(F32) * tap.astype(F32)
                    accs[g] = prod if accs[g] is None else accs[g] + prod
            for g in range(CONV_GROUPS):
                cbuf[pl.ds(r0 + g * GROUP_ROWS, GROUP_ROWS), cols] = accs[g] + bias
            return carry

        lax.fori_loop(0, ts // block_rows, body, 0)

    v = cbuf[...]
    mu = jnp.mean(v, axis=-1, keepdims=True)
    vc = v - mu
    var = jnp.mean(vc * vc, axis=-1, keepdims=True)
    ln = vc * lax.rsqrt(var + EPS) * lng_ref[...] + lnb_ref[...]
    act = (ln * jax.nn.sigmoid(ln)).astype(BF16)
    y = jnp.dot(act, w2_ref[...], preferred_element_type=F32) + b2_ref[...]
    o_ref[0] = x_ref[0] + y * _rms_scale(y) * (gt * post_g_ref[...])
    for j in range(n_slabs):
        ubuf[j, 0:CONV_HALO, :] = ubuf[j, ts:ext, :]


def _conf_layer(x, mod, pre_g, post_g, w1, b1, wdw, bdw, lng, lnb, w2, b2, *, ts):
    bsz, seq, d = x.shape
    kw = wdw.shape[0]
    n_slabs = d // LANES
    ext = CONV_HALO + ts
    assert seq % ts == 0 and ts % (CONV_GROUPS * GROUP_ROWS) == 0
    assert d % MXU_COLS == 0 and kw - 1 <= CONV_HALO
    taps = wdw.astype(BF16).reshape(kw, n_slabs, 1, LANES).transpose(1, 0, 2, 3)
    taps = jnp.broadcast_to(taps, (n_slabs, kw, GROUP_ROWS, LANES))
    taps = taps.reshape(n_slabs, kw * GROUP_ROWS, LANES)
    return pl.pallas_call(
        _conf_kernel,
        grid=(bsz, seq // ts),
        in_specs=[
            _row_tile(ts, d), _mod_spec(d), _resident((1, d)), _resident((1, d)),
            _resident((d, 2 * d)), _resident((1, 2 * d)),
            _resident(taps.shape), _resident((1, d)), _resident((1, d)),
            _resident((1, d)), _resident((d, d)), _resident((1, d)),
        ],
        out_specs=_row_tile(ts, d),
        out_shape=jax.ShapeDtypeStruct(x.shape, x.dtype),
        scratch_shapes=[pltpu.VMEM((n_slabs, ext + CONV_TAIL, LANES), F32),
                        pltpu.VMEM((n_slabs, ext // PACK, LANES), jnp.uint32),
                        pltpu.VMEM((n_slabs, ext // PACK, LANES), jnp.uint32),
                        pltpu.VMEM((ts, d), F32)],
        compiler_params=_seq_params(12, (4, 10)),
        name="conformer_conv",
    )(x, mod, pre_g.reshape(1, d), post_g.reshape(1, d),
      w1.astype(BF16), b1.reshape(1, 2 * d), taps, bdw.reshape(1, d),
      lng.reshape(1, d), lnb.reshape(1, d), w2.astype(BF16), b2.reshape(1, d))


FFN_CHUNK = 256
FFN_TAIL_CHUNKS = 5
FFN_RING = 2


def _ffn_kernel(x_ref, mod_ref, pre_g_ref, post_g_ref, wup_ref, wdw_ref,
                wdn_ref, o_ref, ubuf, carry, abuf):
    s = pl.program_id(1)
    ts = x_ref.shape[1]
    f = wdn_ref.shape[0]
    kw = wdw_ref.shape[0]
    fc = FFN_CHUNK
    slabs = fc // LANES
    n_chunks = f // fc

    @pl.when(s == 0)
    def _():
        carry[...] = jnp.zeros(carry.shape, F32)

    m = mod_ref[0]
    sh, sc, gt = m[3:4], m[4:5], m[5:6]
    in_gain = pre_g_ref[...] * (1.0 + sc)
    out_gain = gt * post_g_ref[...]
    x = x_ref[0]
    h = (x * _rms_scale(x) * in_gain + sh).astype(BF16)

    def up(j):
        for part, off in enumerate((j * fc, f + j * fc)):
            u = jnp.dot(h, wup_ref[:, off:off + fc], preferred_element_type=F32)
            for c in range(slabs):
                ubuf[j % FFN_RING, part, c, FFN_HALO:FFN_HALO + ts, :] = u[:, _lane_block(c)]

    def conv(j, part, c):
        slab = (part * f + j * fc) // LANES + c
        cols = _lane_block(slab)
        buf = ubuf.at[j % FFN_RING, part, c]
        buf[0:FFN_HALO, :] = carry[slab]
        out = None
        for k in range(kw):
            lo = FFN_HALO - (kw - 1) + k
            term = wdw_ref[k:k + 1, cols] * buf[_rows(lo, ts), :]
            out = term if out is None else out + term
        carry[slab] = buf[ts:ts + FFN_HALO, :]
        return out

    def act(j):
        for c in range(slabs):
            gate = conv(j, 0, c)
            val = conv(j, 1, c)
            cols = _lane_block(j * slabs + c)
            abuf[:, cols] = (gate * jax.nn.sigmoid(gate) * val).astype(BF16)

    up(0)
    for j in range(n_chunks):
        if j + 1 < n_chunks:
            up(j + 1)
        act(j)
    split = (n_chunks - FFN_TAIL_CHUNKS) * fc
    o_ref[0] = (jnp.dot(abuf[:, :split], wdn_ref[:split, :], preferred_element_type=F32)
                + jnp.dot(abuf[:, split:], wdn_ref[split:, :], preferred_element_type=F32))
    y = o_ref[0]
    o_ref[0] = x_ref[0] + y * _rms_scale(y) * out_gain


def _ffn_layer(x, mod, pre_g, post_g, w_up, w_dw, w_down, *, ts):
    bsz, seq, d = x.shape
    f = w_down.shape[0]
    kw = w_dw.shape[0]
    assert seq % ts == 0 and f % FFN_CHUNK == 0 and kw - 1 <= FFN_HALO
    assert f // FFN_CHUNK > FFN_TAIL_CHUNKS
    return pl.pallas_call(
        _ffn_kernel,
        grid=(bsz, seq // ts),
        in_specs=[
            _row_tile(ts, d), _mod_spec(d), _resident((1, d)), _resident((1, d)),
            _resident((d, 2 * f)), _resident((kw, 2 * f)), _resident((f, d)),
        ],
        out_specs=_row_tile(ts, d),
        out_shape=jax.ShapeDtypeStruct(x.shape, x.dtype),
        scratch_shapes=[
            pltpu.VMEM((FFN_RING, 2, FFN_CHUNK // LANES, ts + FFN_HALO, LANES), F32),
            pltpu.VMEM((2 * f // LANES, FFN_HALO, LANES), F32),
            pltpu.VMEM((ts, f), BF16)],
        compiler_params=_seq_params(7, (4, 6)),
        name="conv_ffn",
    )(x, mod, pre_g.reshape(1, d), post_g.reshape(1, d),
      w_up.astype(BF16), w_dw, w_down.astype(BF16))


def kernel(x, c, ada_w, ada_b, pre_g, post_g, pool_w, pool_scale, cv_w_pw1, cv_b_pw1, cv_w_dw, cv_b_dw, cv_ln_g, cv_ln_b, cv_w_pw2, cv_b_pw2, ffn_w_up, ffn_w_dw, ffn_w_down):
    depth = ada_w.shape[0]
    mod = _modulation(c, ada_w, ada_b)
    for i in range(depth):
        j = i // 2
        if i % 2 == 0:
            x = _pool_layer(x, mod[i], pre_g[i, 0], post_g[i, 0],
                            pool_w[j], pool_scale[j], ts=ROW_TILE)
        else:
            x = _conf_layer(x, mod[i], pre_g[i, 0], post_g[i, 0],
                            cv_w_pw1[j], cv_b_pw1[j], cv_w_dw[j], cv_b_dw[j],
                            cv_ln_g[j], cv_ln_b[j], cv_w_pw2[j], cv_b_pw2[j],
                            ts=ROW_TILE)
        x = _ffn_layer(x, mod[i], pre_g[i, 1], post_g[i, 1],
                       ffn_w_up[i], ffn_w_dw[i], ffn_w_down[i], ts=ROW_TILE)
    return x
```
